```python
import math
import jax, jax.numpy as jnp
from jax import lax
import numpy as np


D_MODEL = 1024
BATCH = 2
SEQ = 8192
DEPTH = 2
DEC_BATCH = 4
DEC_SEQ = 4096
PAST_LEN = 128

POOL_GROUPS = 4
POOL_GDIM = 64
POOL_W = POOL_GROUPS * POOL_GDIM
POOL_WINDOWS = (2, 4, 8, 16)
SSD_HEADS = 8
SSD_HDIM = 64
SSD_INNER = SSD_HEADS * SSD_HDIM
SSD_GROUPS = 2
SSD_STATE = 128
SSD_CONV = 5
SSD_CHUNK = 128
SSD_XBC = SSD_INNER + 2 * SSD_GROUPS * SSD_STATE
ATT_HEADS = 4
ATT_HDIM = 64
ATT_W = ATT_HEADS * ATT_HDIM
D_MIX = POOL_W + SSD_INNER + ATT_W
GRID_W = 64
NA_ROWS = 8
NA_COLS = 16
SPLITS = (POOL_W, POOL_W + SSD_INNER, POOL_W + SSD_INNER + SSD_XBC, POOL_W + SSD_INNER + SSD_XBC + 2 * SSD_HEADS)
N_IN = SPLITS[-1] + 3 * ATT_W
N_EXPERTS = 32
TOP_K = 4
D_FF = D_MODEL
SWIGLU_LIMIT = 7.0
SWIGLU_ALPHA = 1.702
MOE_BLOCK = 128
DEEPNORM_ALPHA = (2 * DEPTH) ** 0.25
DEEPNORM_BETA = (8 * DEPTH) ** -0.25
LN_EPS = 1e-5
RMS_EPS = 1e-5

kernel_name = 'hybrid_pool_ssd_natt_moe_encoder'


def _layer_norm(x, g, b):
    xf = x.astype(jnp.float32)
    mu = jnp.mean(xf, axis=-1, keepdims=True)
    var = jnp.mean(jnp.square(xf - mu), axis=-1, keepdims=True)
    return ((xf - mu) * lax.rsqrt(var + LN_EPS) * g + b).astype(x.dtype)


def _pool_mixer(u, w_pool, pool_scale):
    b, s, _ = u.shape
    ug = u.reshape(b, s, POOL_GROUPS, POOL_GDIM).astype(jnp.float32)
    cs = jnp.concatenate([jnp.zeros((b, 1, POOL_GROUPS, POOL_GDIM), jnp.float32), jnp.cumsum(ug, axis=1)], axis=1)
    t = jnp.arange(s)[:, None]
    half = jnp.array(POOL_WINDOWS, jnp.int32)[None, :] // 2
    lo = jnp.maximum(t - half, 0)
    hi = jnp.minimum(t + half, s)
    gi = jnp.arange(POOL_GROUPS)[None, :]
    cnt = (hi - lo).astype(jnp.float32)[None, :, :, None]
    pooled = (cs[:, hi, gi] - cs[:, lo, gi]) / cnt - ug
    out = jnp.einsum('bsgc,gcd->bsgd', pooled.astype(u.dtype), w_pool)
    return out.reshape(b, s, POOL_W) * pool_scale


def _dwconv(u, w, bias):
    ch = u.shape[-1]
    out = lax.conv_general_dilated(u, w[:, None, :], window_strides=(1,),
                                   padding=((SSD_CONV // 2, SSD_CONV // 2),),
                                   dimension_numbers=('NWC', 'WIO', 'NWC'),
                                   feature_group_count=ch)
    return out + bias


def _ssd_scan(xh, dt, a_neg, bm, cm):
    b, s, h, p = xh.shape
    g, n = bm.shape[-2], bm.shape[-1]
    hg = h // g
    c, L = s // SSD_CHUNK, SSD_CHUNK
    xc = (xh * dt[..., None]).reshape(b, c, L, g, hg, p)
    bc = bm.reshape(b, c, L, g, n)
    cc = cm.reshape(b, c, L, g, n)
    a_cs = jnp.cumsum((dt * a_neg).reshape(b, c, L, g, hg), axis=2)
    a_t = jnp.transpose(a_cs, (0, 3, 4, 1, 2))
    seg = a_t[..., :, None] - a_t[..., None, :]
    tri = jnp.tril(jnp.ones((L, L), dtype=bool))
    decay = jnp.where(tri, jnp.exp(jnp.where(tri, seg, 0.0)), 0.0)
    cb = jnp.einsum('bclgn,bcsgn->bgcls', cc, bc)
    y_diag = jnp.einsum('bghcls,bcsghp->bclghp', cb[:, :, None] * decay, xc)
    to_end = jnp.exp(a_cs[:, :, -1:] - a_cs)
    states = jnp.einsum('bclgn,bclghp->bcghpn', bc, xc * to_end[..., None])
    chunk_decay = jnp.exp(a_cs[:, :, -1])

    def step(carry, inp):
        st, dec = inp
        return carry * dec[..., None, None] + st, carry

    init = jnp.zeros((b, g, hg, p, n), xc.dtype)
    _, prev = lax.scan(step, init, (jnp.moveaxis(states, 1, 0), jnp.moveaxis(chunk_decay, 1, 0)))
    prev = jnp.moveaxis(prev, 0, 1)
    y_off = jnp.einsum('bclgn,bcghpn->bclghp', cc, prev) * jnp.exp(a_cs)[..., None]
    return (y_diag + y_off).reshape(b, s, h, p)


def _ssd_mixer(z, xbc, dt_raw, conv_w, conv_b, a_log, dt_bias, d_skip, norm_g):
    b, s, _ = z.shape
    f32 = jnp.float32
    xbc = jax.nn.silu(_dwconv(xbc, conv_w, conv_b)).astype(f32)
    xs, bm, cm = jnp.split(xbc, [SSD_INNER, SSD_INNER + SSD_GROUPS * SSD_STATE], axis=-1)
    xh = xs.reshape(b, s, SSD_HEADS, SSD_HDIM)
    bm = bm.reshape(b, s, SSD_GROUPS, SSD_STATE)
    cm = cm.reshape(b, s, SSD_GROUPS, SSD_STATE)
    dt = jax.nn.softplus(dt_raw.astype(f32).reshape(b, s, 2, SSD_HEADS) + dt_bias.astype(f32))
    a_neg = -jnp.exp(a_log.astype(f32))
    y_fwd = _ssd_scan(xh, dt[:, :, 0], a_neg[0], bm, cm)
    rev = lambda t: jnp.flip(t, axis=1)
    y_bwd = rev(_ssd_scan(rev(xh), rev(dt[:, :, 1]), a_neg[1], rev(bm), rev(cm)))
    y = y_fwd + y_bwd + xh * d_skip.astype(f32)[:, None]
    y = y.reshape(b, s, SSD_INNER) * jax.nn.silu(z.astype(f32))
    yg = y.reshape(b, s, SSD_GROUPS, SSD_INNER // SSD_GROUPS)
    yg = yg * lax.rsqrt(jnp.mean(jnp.square(yg), axis=-1, keepdims=True) + RMS_EPS)
    return (yg.reshape(b, s, SSD_INNER) * norm_g).astype(z.dtype)


def _neighbourhood_attention(q, k, v, rpb):
    b, s, h, d = q.shape
    rows = s // GRID_W
    kr = min(NA_ROWS, rows)
    qg = q.reshape(b, rows, GRID_W, h, d)
    kg = k.reshape(b, rows, GRID_W, h, d)
    vg = v.reshape(b, rows, GRID_W, h, d)
    cols = jnp.arange(GRID_W)
    col_idx = jnp.clip(cols - NA_COLS // 2, 0, GRID_W - NA_COLS)[:, None] + jnp.arange(NA_COLS)[None, :]
    bias_c = rpb[:, :, col_idx - cols[:, None] + NA_COLS - 1]
    scale = d ** -0.5

    def row_block(r):
        rs = jnp.clip(r - kr // 2, 0, rows - kr)
        kb = lax.dynamic_slice_in_dim(kg, rs, kr, axis=1)[:, :, col_idx]
        vb = lax.dynamic_slice_in_dim(vg, rs, kr, axis=1)[:, :, col_idx]
        qr = lax.dynamic_index_in_dim(qg, r, axis=1, keepdims=False)
        bias = jnp.take(bias_c, rs + jnp.arange(kr) - r + NA_ROWS - 1, axis=1)
        logits = (jnp.einsum('bqhd,brqjhd->bhqrj', qr, kb).astype(jnp.float32) * scale
                  + jnp.transpose(bias, (0, 2, 1, 3)).astype(jnp.float32))
        prob = jax.nn.softmax(logits.reshape(b, h, GRID_W, kr * NA_COLS), axis=-1)
        prob = prob.reshape(b, h, GRID_W, kr, NA_COLS).astype(v.dtype)
        return jnp.einsum('bhqrj,brqjhd->bqhd', prob, vb)

    out = lax.map(row_block, jnp.arange(rows))
    return jnp.moveaxis(out, 0, 1).reshape(b, s, h * d)


def _mixer(h, w_in, conv_w, conv_b, a_log, dt_bias, d_skip, ssd_norm_g, w_pool, pool_scale, rpb, w_out):
    b, s, _ = h.shape
    proj = jnp.einsum('bsd,de->bse', h, w_in)
    u, z, xbc, dt_raw, qkv = jnp.split(proj, list(SPLITS), axis=-1)
    pool_out = _pool_mixer(u, w_pool, pool_scale)
    ssd_out = _ssd_mixer(z, xbc, dt_raw, conv_w, conv_b, a_log, dt_bias, d_skip, ssd_norm_g)
    qkv = qkv.reshape(b, s, 3, ATT_HEADS, ATT_HDIM)
    att_out = _neighbourhood_attention(qkv[:, :, 0], qkv[:, :, 1], qkv[:, :, 2], rpb)
    cat = jnp.concatenate([pool_out, ssd_out, att_out], axis=-1)
    return jnp.einsum('bse,ed->bsd', cat, w_out)


def _moe(h, w_router, b_router, w1, b1, w2, b2):
    b, s, d = h.shape
    xt = h.reshape(-1, d)
    n_tok = xt.shape[0]
    n_asg = n_tok * TOP_K
    logits = (xt @ w_router).astype(jnp.float32) + b_router.astype(jnp.float32)
    top_v, top_e = lax.top_k(logits, TOP_K)
    gates = jax.nn.softmax(top_v, axis=-1)
    flat_e = top_e.reshape(-1)
    order = jnp.argsort(flat_e)
    se = flat_e[order]
    tok = order // TOP_K
    counts = jnp.bincount(flat_e, length=N_EXPERTS)
    pcounts = (counts + MOE_BLOCK - 1) // MOE_BLOCK * MOE_BLOCK
    pend = jnp.cumsum(pcounts)
    pstart = pend - pcounts
    start = jnp.cumsum(counts) - counts
    dest = pstart[se] + jnp.arange(n_asg) - start[se]
    n_blocks = -(-n_asg // MOE_BLOCK) + N_EXPERTS
    xpad = jnp.zeros((n_blocks * MOE_BLOCK, d), h.dtype).at[dest].set(xt[tok])
    blk_e = jnp.minimum(jnp.searchsorted(pend, jnp.arange(n_blocks) * MOE_BLOCK, side='right'), N_EXPERTS - 1)

    def expert_block(args):
        xb, e = args
        hid = xb @ w1[e] + b1[e]
        h_glu = jnp.minimum(hid[:, 0::2], SWIGLU_LIMIT)
        h_lin = jnp.clip(hid[:, 1::2], -SWIGLU_LIMIT, SWIGLU_LIMIT)
        act = h_glu * jax.nn.sigmoid(SWIGLU_ALPHA * h_glu) * (h_lin + 1.0)
        return act @ w2[e] + b2[e]

    ypad = lax.map(expert_block, (xpad.reshape(n_blocks, MOE_BLOCK, d), blk_e)).reshape(-1, d)
    wgt = gates.reshape(-1)[order].astype(h.dtype)
    out = jnp.zeros_like(xt).at[tok].add(ypad[dest] * wgt[:, None])
    return out.reshape(b, s, d)


def _encoder(x, params):
    (w_in, conv_w, conv_b, a_log, dt_bias, d_skip, ssd_norm_g, w_pool, pool_scale, rpb, w_out,
     ln1_g, ln1_b, w_router, b_router, w1, b1, w2, b2, ln2_g, ln2_b) = params
    h = x
    for i in range(DEPTH):
        m = _mixer(h, w_in[i], conv_w[i], conv_b[i], a_log[i], dt_bias[i], d_skip[i], ssd_norm_g[i],
                   w_pool[i], pool_scale[i], rpb[i], w_out[i])
        h = _layer_norm(DEEPNORM_ALPHA * h + m, ln1_g[i], ln1_b[i])
        f = _moe(h, w_router[i], b_router[i], w1[i], b1[i], w2[i], b2[i])
        h = _layer_norm(DEEPNORM_ALPHA * h + f, ln2_g[i], ln2_b[i])
    return h


def setup_inputs(seed: int = 0) -> dict:
    key = jax.random.key(seed)
    ks = jax.random.split(key, 24)
    f32 = jnp.float32
    nrm = lambda k, shape, sc: sc * jax.random.normal(k, shape, f32)
    L = DEPTH
    dt0 = jnp.exp(jax.random.uniform(ks[6], (L, 2, SSD_HEADS), f32, minval=math.log(1e-3), maxval=math.log(1e-1)))
    return {
        'x_prompt': nrm(ks[0], (BATCH, SEQ, D_MODEL), 1.0),
        'x_sample': nrm(ks[1], (DEC_BATCH, DEC_SEQ, D_MODEL), 1.0),
        'w_in': nrm(ks[2], (L, D_MODEL, N_IN), D_MODEL ** -0.5),
        'conv_w': nrm(ks[3], (L, SSD_CONV, SSD_XBC), SSD_CONV ** -0.5),
        'conv_b': nrm(ks[4], (L, SSD_XBC), 0.01),
        'a_log': jnp.log(jax.random.uniform(ks[5], (L, 2, SSD_HEADS), f32, minval=1.0, maxval=16.0)),
        'dt_bias': dt0 + jnp.log(-jnp.expm1(-dt0)),
        'd_skip': 1.0 + nrm(ks[7], (L, SSD_HEADS), 0.1),
        'ssd_norm_g': 1.0 + nrm(ks[8], (L, SSD_INNER), 0.02),
        'w_pool': nrm(ks[9], (L, POOL_GROUPS, POOL_GDIM, POOL_GDIM), POOL_GDIM ** -0.5),
        'pool_scale': 1.0 + nrm(ks[10], (L, POOL_W), 0.1),
        'rpb': nrm(ks[11], (L, ATT_HEADS, 2 * NA_ROWS - 1, 2 * NA_COLS - 1), 0.1),
        'w_out': nrm(ks[12], (L, D_MIX, D_MODEL), D_MIX ** -0.5 * DEEPNORM_BETA),
        'ln1_g': 1.0 + nrm(ks[13], (L, D_MODEL), 0.02),
        'ln1_b': nrm(ks[14], (L, D_MODEL), 0.02),
        'w_router': nrm(ks[15], (L, D_MODEL, N_EXPERTS), D_MODEL ** -0.5),
        'b_router': nrm(ks[16], (L, N_EXPERTS), 0.01),
        'w1': nrm(ks[17], (L, N_EXPERTS, D_MODEL, 2 * D_FF), D_MODEL ** -0.5),
        'b1': nrm(ks[18], (L, N_EXPERTS, 2 * D_FF), 0.01),
        'w2': nrm(ks[19], (L, N_EXPERTS, D_FF, D_MODEL), D_FF ** -0.5 * DEEPNORM_BETA),
        'b2': nrm(ks[20], (L, N_EXPERTS, D_MODEL), 0.01),
        'ln2_g': 1.0 + nrm(ks[21], (L, D_MODEL), 0.02),
        'ln2_b': nrm(ks[22], (L, D_MODEL), 0.02),
    }


def reference(x_prompt, x_sample, w_in, conv_w, conv_b, a_log, dt_bias, d_skip, ssd_norm_g, w_pool,
              pool_scale, rpb, w_out, ln1_g, ln1_b, w_router, b_router, w1, b1, w2, b2, ln2_g, ln2_b):
    params = (w_in, conv_w, conv_b, a_log, dt_bias, d_skip, ssd_norm_g, w_pool, pool_scale, rpb, w_out,
              ln1_g, ln1_b, w_router, b_router, w1, b1, w2, b2, ln2_g, ln2_b)
    y_prompt = _encoder(x_prompt, params)
    y_sample = _encoder(x_sample, params)
    return (y_prompt, y_sample)
```

```python
import functools

import numpy as np
import jax
import jax.numpy as jnp
from jax import lax
from jax.experimental import pallas as pl
from jax.experimental.pallas import tpu as pltpu

F32 = jnp.float32
BF16 = jnp.bfloat16

D_MODEL = 1024
POOL_GROUPS = 4
POOL_GDIM = 64
POOL_W = POOL_GROUPS * POOL_GDIM
POOL_WINDOWS = (2, 4, 8, 16)
SSD_HEADS = 8
SSD_HDIM = 64
SSD_INNER = SSD_HEADS * SSD_HDIM
SSD_GROUPS = 2
SSD_STATE = 128
SSD_CONV = 5
SSD_CHUNK = 128
SSD_BC = 2 * SSD_GROUPS * SSD_STATE
SSD_XBC = SSD_INNER + SSD_BC
ATT_HEADS = 4
ATT_HDIM = 64
ATT_W = ATT_HEADS * ATT_HDIM
GRID_W = 64
NA_ROWS = 8
NA_COLS = 16
N_EXPERTS = 32
TOP_K = 4
D_FF = D_MODEL
SWIGLU_LIMIT = 7.0
SWIGLU_ALPHA = 1.702
LN_EPS = 1e-5
RMS_EPS = 1e-5

DT_PAD = 128
HALO = 8
NEG_BIG = -1e30
NA_KEYS = NA_ROWS * GRID_W

TM_PROJ = 512
TS_SEQ = 256
ATT_ROWS_PER_STEP = 8
MOE_BM = 256
FF_CHUNK = 512
VMEM_LIMIT = 56 * 1024 * 1024


def _cparams(sem):
    return pltpu.CompilerParams(dimension_semantics=sem, vmem_limit_bytes=VMEM_LIMIT)


def _dot(a, b):
    return jnp.dot(a, b, preferred_element_type=F32)


def _dot_nt(a, b, precision=None):
    return lax.dot_general(a, b, (((1,), (1,)), ((), ())), preferred_element_type=F32, precision=precision)


def _inproj_kernel(x_ref, w_ref, wdt_ref, u_ref, z_ref, xbc_ref, q_ref, k_ref, v_ref, dt_ref, dtt_ref):
    xb = x_ref[...].astype(BF16)
    c0 = 0
    for ref, width in ((u_ref, POOL_W), (z_ref, SSD_INNER), (xbc_ref, SSD_XBC), (q_ref, ATT_W),
                       (k_ref, ATT_W), (v_ref, ATT_W), (dt_ref, DT_PAD)):
        ref[...] = _dot(xb, w_ref[:, c0:c0 + width]).astype(ref.dtype)
        c0 += width
    dtt_ref[...] = _dot_nt(wdt_ref[...], xb)


def _in_proj(h, w_all, w_dt_t):
    t = h.shape[0]
    tm = TM_PROJ
    n_all = w_all.shape[1]
    row = lambda w: pl.BlockSpec((tm, w), lambda i: (i, 0))
    full = lambda a: pl.BlockSpec(a.shape, lambda i: (0,) * a.ndim)
    return pl.pallas_call(
        _inproj_kernel,
        grid=(t // tm,),
        in_specs=[row(D_MODEL), full(w_all), full(w_dt_t)],
        out_specs=[row(POOL_W), row(SSD_INNER), row(SSD_XBC), row(ATT_W), row(ATT_W), row(ATT_W), row(DT_PAD),
                   pl.BlockSpec((2 * SSD_HEADS, tm), lambda i: (0, i))],
        out_shape=[jax.ShapeDtypeStruct((t, POOL_W), F32), jax.ShapeDtypeStruct((t, SSD_INNER), F32),
                   jax.ShapeDtypeStruct((t, SSD_XBC), F32), jax.ShapeDtypeStruct((t, ATT_W), BF16),
                   jax.ShapeDtypeStruct((t, ATT_W), BF16), jax.ShapeDtypeStruct((t, ATT_W), BF16),
                   jax.ShapeDtypeStruct((t, DT_PAD), F32), jax.ShapeDtypeStruct((2 * SSD_HEADS, t), F32)],
        compiler_params=_cparams(("parallel",)),
        name="in_proj",
    )(h, w_all, w_dt_t)


def _seq_tables(seq_lens, tile):
    pos, length = [], []
    for s in seq_lens:
        assert s % tile == 0
        for p in range(0, s, tile):
            pos.append(p)
            length.append(s)
    return np.asarray(pos, np.int32), np.asarray(length, np.int32)


def _halo_specs(ts, width, n_tok):
    hb = ts // HALO
    last = n_tok // HALO - 1
    cur = pl.BlockSpec((ts, width), lambda i, *_: (i, 0))
    prev = pl.BlockSpec((HALO, width), lambda i, *_: (jnp.maximum(i * hb - 1, 0), 0))
    nxt = pl.BlockSpec((HALO, width), lambda i, *_: (jnp.minimum((i + 1) * hb, last), 0))
    return prev, cur, nxt


def _extended(prev_ref, cur_ref, next_ref, pos0, slen, ts):
    prev = jnp.where(pos0 > 0, prev_ref[...], 0.0)
    nxt = jnp.where(pos0 + ts < slen, next_ref[...], 0.0)
    return jnp.concatenate([prev, cur_ref[...], nxt], axis=0)


def _rows_from(ext, k, ts):
    n = ext.shape[0]
    shift = (n - (HALO + k)) % n
    rolled = ext if shift == 0 else pltpu.roll(ext, shift, 0)
    return rolled[:ts]


def _pool_kernel(pos_ref, len_ref, prev_ref, cur_ref, next_ref, w_ref, scale_ref, o_ref, *, ts):
    i = pl.program_id(0)
    pos0 = pos_ref[i]
    slen = len_ref[i]
    ext = _extended(prev_ref, cur_ref, next_ref, pos0, slen, ts)
    n = ext.shape[0]

    def pair_sum(a, k):
        return a + pltpu.roll(a, n - k, 0)

    a2 = pair_sum(ext, 1)
    a4 = pair_sum(a2, 2)
    a8 = pair_sum(a4, 4)
    a16 = pair_sum(a8, 8)
    sums = [_rows_from(a, -w // 2, ts) for a, w in zip((a2, a4, a8, a16), POOL_WINDOWS)]

    lane = lax.broadcasted_iota(jnp.int32, (ts, POOL_W), 1)
    grp = lane // POOL_GDIM
    wsum = jnp.where(grp == 0, sums[0], jnp.where(grp == 1, sums[1], jnp.where(grp == 2, sums[2], sums[3])))
    half = jnp.where(grp == 0, 1, jnp.where(grp == 1, 2, jnp.where(grp == 2, 4, 8)))
    tpos = pos0 + lax.broadcasted_iota(jnp.int32, (ts, POOL_W), 0)
    cnt = (jnp.minimum(tpos + half, slen) - jnp.maximum(tpos - half, 0)).astype(F32)
    pooled = wsum / cnt - cur_ref[...]
    o_ref[...] = _dot(pooled.astype(BF16), w_ref[...]) * scale_ref[...]


def _pool_mixer(u, w_bd, scale, seq_lens):
    t = u.shape[0]
    ts = TS_SEQ
    pos, length = _seq_tables(seq_lens, ts)
    prev, cur, nxt = _halo_specs(ts, POOL_W, t)
    grid_spec = pltpu.PrefetchScalarGridSpec(
        num_scalar_prefetch=2, grid=(t // ts,),
        in_specs=[prev, cur, nxt,
                  pl.BlockSpec((POOL_W, POOL_W), lambda i, *_: (0, 0)),
                  pl.BlockSpec((1, POOL_W), lambda i, *_: (0, 0))],
        out_specs=pl.BlockSpec((ts, POOL_W), lambda i, *_: (i, 0)))
    return pl.pallas_call(
        functools.partial(_pool_kernel, ts=ts), grid_spec=grid_spec,
        out_shape=jax.ShapeDtypeStruct((t, POOL_W), F32),
        compiler_params=_cparams(("parallel",)), name="pool_mixer",
    )(pos, length, u, u, u, w_bd, scale)


def _conv_kernel(pos_ref, len_ref, prev_ref, cur_ref, next_ref, w_ref, b_ref, xs_ref, bc_ref, *, ts):
    i = pl.program_id(0)
    ext = _extended(prev_ref, cur_ref, next_ref, pos_ref[i], len_ref[i], ts)
    acc = jnp.zeros((ts, SSD_XBC), F32) + b_ref[...]
    for j in range(SSD_CONV):
        acc = acc + _rows_from(ext, j - SSD_CONV // 2, ts) * w_ref[j:j + 1, :]
    act = acc * jax.nn.sigmoid(acc)
    xs_ref[...] = act[:, :SSD_INNER]
    bc_ref[...] = act[:, SSD_INNER:]


def _conv_silu(xbc, conv_w, conv_b, seq_lens):
    t = xbc.shape[0]
    ts = TS_SEQ
    pos, length = _seq_tables(seq_lens, ts)
    prev, cur, nxt = _halo_specs(ts, SSD_XBC, t)
    grid_spec = pltpu.PrefetchScalarGridSpec(
        num_scalar_prefetch=2, grid=(t // ts,),
        in_specs=[prev, cur, nxt,
                  pl.BlockSpec((HALO, SSD_XBC), lambda i, *_: (0, 0)),
                  pl.BlockSpec((1, SSD_XBC), lambda i, *_: (0, 0))],
        out_specs=[pl.BlockSpec((ts, SSD_INNER), lambda i, *_: (i, 0)),
                   pl.BlockSpec((ts, SSD_BC), lambda i, *_: (i, 0))])
    return pl.pallas_call(
        functools.partial(_conv_kernel, ts=ts), grid_spec=grid_spec,
        out_shape=[jax.ShapeDtypeStruct((t, SSD_INNER), F32), jax.ShapeDtypeStruct((t, SSD_BC), F32)],
        compiler_params=_cparams(("parallel",)), name="conv_silu",
    )(pos, length, xbc, xbc, xbc, conv_w, conv_b)


def _ssd_chunk(xs, bc, dt_col, dt_row, aneg_row, aneg_col, state_ref, reset, reverse):
    L = SSD_CHUNK
    hp = jax.lax.Precision.HIGHEST
    ii = lax.broadcasted_iota(jnp.int32, (L, L), 0)
    jj = lax.broadcasted_iota(jnp.int32, (L, L), 1)
    tri = (jj >= ii) if reverse else (jj <= ii)
    tri_t = (ii >= jj) if reverse else (ii <= jj)
    a_col = dt_col * aneg_row
    a_row = dt_row * aneg_col
    acs_col = jnp.dot(tri.astype(F32), a_col, preferred_element_type=F32, precision=hp)
    acs_row = jnp.dot(a_row, tri_t.astype(F32), preferred_element_type=F32, precision=hp)
    total = acs_col[0:1, :] if reverse else acs_col[L - 1:L, :]
    to_end = jnp.exp(total - acs_col)
    exp_acs = jnp.exp(acs_col)
    chunk_decay = jnp.exp(total)

    ys = []
    for g in range(SSD_GROUPS):
        b_g = bc[:, g * SSD_STATE:(g + 1) * SSD_STATE].astype(BF16)
        c_g = bc[:, (SSD_GROUPS + g) * SSD_STATE:(SSD_GROUPS + g + 1) * SSD_STATE].astype(BF16)
        cb = _dot_nt(c_g, b_g)
        for hh in range(SSD_HEADS // SSD_GROUPS):
            h = g * (SSD_HEADS // SSD_GROUPS) + hh
            seg = acs_col[:, h:h + 1] - acs_row[h:h + 1, :]
            decay = jnp.where(tri, jnp.exp(jnp.where(tri, seg, 0.0)), 0.0)
            xc = xs[:, h * SSD_HDIM:(h + 1) * SSD_HDIM] * dt_col[:, h:h + 1]
            y_diag = _dot((cb * decay).astype(BF16), xc.astype(BF16))
            prev = jnp.where(reset, 0.0, state_ref[h])
            y_off = _dot_nt(c_g, prev.astype(BF16)) * exp_acs[:, h:h + 1]
            xw_t = (xc * to_end[:, h:h + 1]).T.astype(BF16)
            st = _dot(xw_t, b_g)
            state_ref[h] = prev * chunk_decay[:, h:h + 1] + st
            ys.append(y_diag + y_off)
    return jnp.concatenate(ys, axis=1)


def _ssd_kernel(mir_ref, start_ref, xs_f, bc_f, dt_f, dtt_f, xs_b, bc_b, dt_b, dtt_b,
                bias_row, bias_col, alog_row, alog_col, yf_ref, yb_ref, st_f, st_b):
    c = pl.program_id(0)
    reset = start_ref[c] == 1
    h = SSD_HEADS
    softplus = lambda v: jnp.maximum(v, 0.0) + jnp.log1p(jnp.exp(-jnp.abs(v)))
    aneg_row = -jnp.exp(alog_row[...])
    aneg_col = -jnp.exp(alog_col[...])
    dtc_f = softplus(dt_f[:, 0:2 * h] + bias_row[...])
    dtr_f = softplus(dtt_f[...] + bias_col[...])
    dtc_b = softplus(dt_b[:, 0:2 * h] + bias_row[...])
    dtr_b = softplus(dtt_b[...] + bias_col[...])
    yf_ref[...] = _ssd_chunk(xs_f[...], bc_f[...], dtc_f[:, :h], dtr_f[:h, :], aneg_row[:, :h], aneg_col[:h, :],
                             st_f, reset, False)
    yb_ref[...] = _ssd_chunk(xs_b[...], bc_b[...], dtc_b[:, h:], dtr_b[h:, :], aneg_row[:, h:], aneg_col[h:, :],
                             st_b, reset, True)


def _ssd_tables(seq_lens):
    mirror, start = [], []
    c0 = 0
    for s in seq_lens:
        assert s % SSD_CHUNK == 0
        n = s // SSD_CHUNK
        for c in range(n):
            mirror.append(c0 + n - 1 - c)
            start.append(1 if c == 0 else 0)
        c0 += n
    return np.asarray(mirror, np.int32), np.asarray(start, np.int32)


def _ssd_scan(xs, bc, dt, dtt, dt_bias, a_log, seq_lens):
    t = xs.shape[0]
    L = SSD_CHUNK
    mirror, start = _ssd_tables(seq_lens)
    fw = lambda w: pl.BlockSpec((L, w), lambda c, m, s: (c, 0))
    bw = lambda w: pl.BlockSpec((L, w), lambda c, m, s: (m[c], 0))
    const = lambda shape: pl.BlockSpec(shape, lambda c, m, s: (0, 0))
    h2 = 2 * SSD_HEADS
    grid_spec = pltpu.PrefetchScalarGridSpec(
        num_scalar_prefetch=2, grid=(t // L,),
        in_specs=[fw(SSD_INNER), fw(SSD_BC), fw(DT_PAD), pl.BlockSpec((h2, L), lambda c, m, s: (0, c)),
                  bw(SSD_INNER), bw(SSD_BC), bw(DT_PAD), pl.BlockSpec((h2, L), lambda c, m, s: (0, m[c])),
                  const((1, h2)), const((h2, 1)), const((1, h2)), const((h2, 1))],
        out_specs=[fw(SSD_INNER), bw(SSD_INNER)],
        scratch_shapes=[pltpu.VMEM((SSD_HEADS, SSD_HDIM, SSD_STATE), F32),
                        pltpu.VMEM((SSD_HEADS, SSD_HDIM, SSD_STATE), F32)])
    return pl.pallas_call(
        _ssd_kernel, grid_spec=grid_spec,
        out_shape=[jax.ShapeDtypeStruct((t, SSD_INNER), F32), jax.ShapeDtypeStruct((t, SSD_INNER), F32)],
        compiler_params=_cparams(("arbitrary",)), name="ssd_scan",
    )(mirror, start, xs, bc, dt, dtt, xs, bc, dt, dtt,
      dt_bias.reshape(1, h2), dt_bias.reshape(h2, 1), a_log.reshape(1, h2), a_log.reshape(h2, 1))


def _att_bias_table(rpb):
    v = np.arange(NA_ROWS)[:, None]
    i = np.arange(NA_ROWS)[None, :]
    dr = i - v + NA_ROWS - 1
    c = np.arange(GRID_W)[:, None]
    kc = np.arange(GRID_W)[None, :]
    cs = np.clip(c - NA_COLS // 2, 0, GRID_W - NA_COLS)
    valid = (kc >= cs) & (kc < cs + NA_COLS)
    dc = np.clip(kc - c + NA_COLS - 1, 0, 2 * NA_COLS - 2)
    tab = rpb[:, dr[:, :, None, None], dc[None, None, :, :]]
    tab = jnp.where(valid[None, None, None], tab.astype(F32), NEG_BIG)
    tab = jnp.transpose(tab, (1, 0, 3, 2, 4))
    return tab.reshape(NA_ROWS, ATT_HEADS, GRID_W, NA_KEYS)


def _att_kernel(q_ref, k_ref, v_ref, bias_ref, o_ref, *, rows, rb):
    j = pl.program_id(1)
    scale = ATT_HDIM ** -0.5
    for rr in range(rb):
        r = j * rb + rr
        rs = jnp.clip(r - NA_ROWS // 2, 0, rows - NA_ROWS)
        voff = r - rs
        k0 = pl.multiple_of(rs * GRID_W, GRID_W)
        kwin = k_ref[pl.ds(k0, NA_KEYS), :]
        vwin = v_ref[pl.ds(k0, NA_KEYS), :]
        q = q_ref[rr * GRID_W:(rr + 1) * GRID_W, :]
        outs = []
        for h in range(ATT_HEADS):
            sl = slice(h * ATT_HDIM, (h + 1) * ATT_HDIM)
            s = _dot_nt(q[:, sl], kwin[:, sl]) * scale + bias_ref[voff, h]
            m = jnp.max(s, axis=-1, keepdims=True)
            p = jnp.exp(s - m)
            den = jnp.sum(p, axis=-1, keepdims=True)
            outs.append(_dot(p.astype(BF16), vwin[:, sl]) / den)
        o_ref[rr * GRID_W:(rr + 1) * GRID_W, :] = jnp.concatenate(outs, axis=1).astype(o_ref.dtype)


def _attention(q, k, v, bias_tab, tok0, n_seq, seq_len):
    rows = seq_len // GRID_W
    assert rows >= NA_ROWS and tok0 % seq_len == 0
    rb = ATT_ROWS_PER_STEP
    steps = rows // rb
    s0 = tok0 // seq_len
    qb0 = tok0 // (rb * GRID_W)
    return pl.pallas_call(
        functools.partial(_att_kernel, rows=rows, rb=rb),
        grid=(n_seq, steps),
        in_specs=[pl.BlockSpec((rb * GRID_W, ATT_W), lambda b, j: (qb0 + b * steps + j, 0)),
                  pl.BlockSpec((seq_len, ATT_W), lambda b, j: (s0 + b, 0)),
                  pl.BlockSpec((seq_len, ATT_W), lambda b, j: (s0 + b, 0)),
                  pl.BlockSpec(bias_tab.shape, lambda b, j: (0, 0, 0, 0))],
        out_specs=pl.BlockSpec((rb * GRID_W, ATT_W), lambda b, j: (b * steps + j, 0)),
        out_shape=jax.ShapeDtypeStruct((n_seq * seq_len, ATT_W), BF16),
        compiler_params=_cparams(("parallel", "parallel")), name="nbr_attention",
    )(q, k, v, bias_tab)


def _layer_norm(x, g, b):
    mu = jnp.mean(x, axis=-1, keepdims=True)
    xc = x - mu
    var = jnp.mean(xc * xc, axis=-1, keepdims=True)
    return xc * lax.rsqrt(var + LN_EPS) * g + b


def _outproj_kernel(h_ref, pool_ref, yf_ref, yb_ref, xs_ref, z_ref, att_ref, wo_ref, dskip_ref, ng_ref,
                    g_ref, b_ref, wr_ref, br_ref, h1_ref, h1b_ref, idx_ref, gate_ref, *, alpha):
    y = yf_ref[...] + yb_ref[...] + xs_ref[...] * dskip_ref[...]
    z = z_ref[...]
    y = y * (z * jax.nn.sigmoid(z))
    gw = SSD_INNER // SSD_GROUPS
    parts = []
    for g in range(SSD_GROUPS):
        yg = y[:, g * gw:(g + 1) * gw]
        parts.append(yg * lax.rsqrt(jnp.mean(yg * yg, axis=-1, keepdims=True) + RMS_EPS))
    ssd = (jnp.concatenate(parts, axis=1) * ng_ref[...]).astype(BF16)
    m = (_dot(pool_ref[...].astype(BF16), wo_ref[0:POOL_W, :])
         + _dot(ssd, wo_ref[POOL_W:POOL_W + SSD_INNER, :])
         + _dot(att_ref[...], wo_ref[POOL_W + SSD_INNER:, :]))
    h1 = _layer_norm(alpha * h_ref[...] + m, g_ref[...], b_ref[...])
    h1_ref[...] = h1
    h1b_ref[...] = h1.astype(BF16)

    logits = _dot_nt(wr_ref[...], h1, precision=jax.lax.Precision.HIGHEST) + br_ref[...]
    eidx = lax.broadcasted_iota(jnp.int32, logits.shape, 0)
    vals, idxs = [], []
    cur = logits
    for _ in range(TOP_K):
        mx = jnp.max(cur, axis=0, keepdims=True)
        ix = jnp.min(jnp.where(cur == mx, eidx, N_EXPERTS), axis=0, keepdims=True)
        vals.append(mx)
        idxs.append(ix)
        cur = jnp.where(eidx == ix, -jnp.inf, cur)
    es = [jnp.exp(vv - vals[0]) for vv in vals]
    den = es[0] + es[1] + es[2] + es[3]
    pad_i = jnp.zeros((8 - TOP_K, logits.shape[1]), jnp.int32)
    pad_f = jnp.zeros((8 - TOP_K, logits.shape[1]), F32)
    idx_ref[...] = jnp.concatenate(idxs + [pad_i], axis=0)
    gate_ref[...] = jnp.concatenate([e / den for e in es] + [pad_f], axis=0)


def _out_proj_router(h, pool_out, y_f, y_b, xs, z, att, w_out, dskip, norm_g, ln_g, ln_b, w_router_t, b_router, alpha):
    t = h.shape[0]
    tm = TM_PROJ // 2
    row = lambda w: pl.BlockSpec((tm, w), lambda i: (i, 0))
    full = lambda a: pl.BlockSpec(a.shape, lambda i: (0,) * a.ndim)
    consts = (w_out, dskip, norm_g, ln_g, ln_b, w_router_t, b_router)
    return pl.pallas_call(
        functools.partial(_outproj_kernel, alpha=alpha),
        grid=(t // tm,),
        in_specs=[row(D_MODEL), row(POOL_W), row(SSD_INNER), row(SSD_INNER), row(SSD_INNER), row(SSD_INNER),
                  row(ATT_W)] + [full(a) for a in consts],
        out_specs=[row(D_MODEL), row(D_MODEL), pl.BlockSpec((8, tm), lambda i: (0, i)),
                   pl.BlockSpec((8, tm), lambda i: (0, i))],
        out_shape=[jax.ShapeDtypeStruct((t, D_MODEL), F32), jax.ShapeDtypeStruct((t, D_MODEL), BF16),
                   jax.ShapeDtypeStruct((8, t), jnp.int32), jax.ShapeDtypeStruct((8, t), F32)],
        compiler_params=_cparams(("parallel",)), name="out_proj_router",
    )(h, pool_out, y_f, y_b, xs, z, att, *consts)


def _expert_kernel(be_ref, nu_ref, x_ref, gate_ref, w1g_ref, w1l_ref, b1g_ref, b1l_ref, w2_ref, b2_ref, o_ref):
    i = pl.program_id(0)

    @pl.when(i < nu_ref[0])
    def _():
        x = x_ref[...]
        acc = jnp.zeros(o_ref.shape, F32) + b2_ref[...]
        for c in range(0, D_FF, FF_CHUNK):
            hg = _dot(x, w1g_ref[:, c:c + FF_CHUNK]) + b1g_ref[:, c:c + FF_CHUNK]
            hl = _dot(x, w1l_ref[:, c:c + FF_CHUNK]) + b1l_ref[:, c:c + FF_CHUNK]
            hg = jnp.minimum(hg, SWIGLU_LIMIT)
            hl = jnp.clip(hl, -SWIGLU_LIMIT, SWIGLU_LIMIT)
            act = hg * jax.nn.sigmoid(SWIGLU_ALPHA * hg) * (hl + 1.0)
            acc = acc + _dot(act.astype(BF16), w2_ref[c:c + FF_CHUNK, :])
        o_ref[...] = acc * gate_ref[...]

    @pl.when(i >= nu_ref[0])
    def _():
        o_ref[...] = jnp.zeros(o_ref.shape, F32)


def _experts(xpad, gate_slot, blk_e, n_used, w1g, w1l, b1g, b1l, w2, b2):
    n_slots = xpad.shape[0]
    bm = MOE_BM
    wspec = lambda r, c: pl.BlockSpec((None, r, c), lambda i, be, nu: (be[i], 0, 0))
    grid_spec = pltpu.PrefetchScalarGridSpec(
        num_scalar_prefetch=2, grid=(n_slots // bm,),
        in_specs=[pl.BlockSpec((bm, D_MODEL), lambda i, be, nu: (i, 0)),
                  pl.BlockSpec((bm, 1), lambda i, be, nu: (i, 0)),
                  wspec(D_MODEL, D_FF), wspec(D_MODEL, D_FF), wspec(1, D_FF), wspec(1, D_FF),
                  wspec(D_FF, D_MODEL), wspec(1, D_MODEL)],
        out_specs=pl.BlockSpec((bm, D_MODEL), lambda i, be, nu: (i, 0)))
    return pl.pallas_call(
        _expert_kernel, grid_spec=grid_spec,
        out_shape=jax.ShapeDtypeStruct((n_slots, D_MODEL), F32),
        compiler_params=_cparams(("arbitrary",)), name="moe_experts",
    )(blk_e, n_used, xpad, gate_slot, w1g, w1l, b1g, b1l, w2, b2)


def _combine_kernel(h_ref, y_ref, g_ref, b_ref, o_ref, *, alpha):
    f = y_ref[0] + y_ref[1] + y_ref[2] + y_ref[3]
    o_ref[...] = _layer_norm(alpha * h_ref[...] + f, g_ref[...], b_ref[...])


def _combine_ln(h1, y_tok, ln_g, ln_b, alpha):
    t = h1.shape[0]
    tm = TM_PROJ // 2
    return pl.pallas_call(
        functools.partial(_combine_kernel, alpha=alpha),
        grid=(t // tm,),
        in_specs=[pl.BlockSpec((tm, D_MODEL), lambda i: (i, 0)),
                  pl.BlockSpec((TOP_K, tm, D_MODEL), lambda i: (0, i, 0)),
                  pl.BlockSpec((1, D_MODEL), lambda i: (0, 0)),
                  pl.BlockSpec((1, D_MODEL), lambda i: (0, 0))],
        out_specs=pl.BlockSpec((tm, D_MODEL), lambda i: (i, 0)),
        out_shape=jax.ShapeDtypeStruct((t, D_MODEL), F32),
        compiler_params=_cparams(("parallel",)), name="combine_ln",
    )(h1, y_tok, ln_g, ln_b)


def _routing(top_e, gates):
    t = top_e.shape[0]
    n_asg = t * TOP_K
    bm = MOE_BM
    n_blocks = n_asg // bm + N_EXPERTS
    flat_e = top_e.reshape(-1)
    onehot = (flat_e[:, None] == jnp.arange(N_EXPERTS, dtype=jnp.int32)[None, :]).astype(jnp.int32)
    incl = jnp.cumsum(onehot, axis=0)
    counts = incl[-1]
    rank = jnp.sum((incl - onehot) * onehot, axis=1)
    pcounts = (counts + bm - 1) // bm * bm
    pend = jnp.cumsum(pcounts)
    pstart = pend - pcounts
    start = jnp.cumsum(counts) - counts
    dest = pstart[flat_e] + rank
    order = jnp.argsort(flat_e * n_asg + jnp.arange(n_asg, dtype=jnp.int32))
    blk_e = jnp.minimum(jnp.searchsorted(pend, jnp.arange(n_blocks, dtype=jnp.int32) * bm, side='right'),
                        N_EXPERTS - 1).astype(jnp.int32)
    slot = jnp.arange(n_blocks * bm, dtype=jnp.int32)
    e_slot = jnp.repeat(blk_e, bm)
    j = slot - pstart[e_slot]
    valid = (j >= 0) & (j < counts[e_slot])
    asg = order[jnp.clip(start[e_slot] + j, 0, n_asg - 1)]
    src_tok = jnp.where(valid, asg // TOP_K, 0)
    gate_slot = jnp.where(valid, gates.reshape(-1)[asg], 0.0)
    n_used = (pend[-1] // bm).astype(jnp.int32).reshape(1)
    return src_tok, gate_slot.reshape(-1, 1), blk_e, n_used, dest.reshape(t, TOP_K)


def _prep_layer(i, w_in, conv_w, conv_b, w_pool, pool_scale, rpb, w_out, w_router, b_router, w1, b1, w2, b2,
                d_skip, ssd_norm_g, ln1_g, ln1_b, ln2_g, ln2_b):
    c_u, c_z, c_x, c_dt = POOL_W, POOL_W + SSD_INNER, POOL_W + SSD_INNER + SSD_XBC, POOL_W + SSD_INNER + SSD_XBC + 2 * SSD_HEADS
    w = w_in[i]
    w_dt = w[:, c_x:c_dt]
    w_all = jnp.concatenate([w[:, :c_x], w[:, c_dt:], w_dt, jnp.zeros((D_MODEL, DT_PAD - 2 * SSD_HEADS), F32)],
                            axis=1).astype(BF16)
    w_bd = jnp.zeros((POOL_W, POOL_W), F32)
    for g in range(POOL_GROUPS):
        sl = slice(g * POOL_GDIM, (g + 1) * POOL_GDIM)
        w_bd = w_bd.at[sl, sl].set(w_pool[i, g])
    conv_w_pad = jnp.concatenate([conv_w[i], jnp.zeros((HALO - SSD_CONV, SSD_XBC), F32)], axis=0)
    return dict(
        w_all=w_all, w_dt_t=w_dt.T.astype(BF16), w_bd=w_bd.astype(BF16), pool_scale=pool_scale[i].reshape(1, POOL_W),
        conv_w=conv_w_pad, conv_b=conv_b[i].reshape(1, SSD_XBC), bias_tab=_att_bias_table(rpb[i]),
        w_out=w_out[i].astype(BF16), dskip=jnp.repeat(d_skip[i], SSD_HDIM).reshape(1, SSD_INNER),
        norm_g=ssd_norm_g[i].reshape(1, SSD_INNER), ln1_g=ln1_g[i].reshape(1, D_MODEL), ln1_b=ln1_b[i].reshape(1, D_MODEL),
        w_router_t=w_router[i].T, b_router=b_router[i].reshape(N_EXPERTS, 1),
        w1g=w1[i, :, :, 0::2].astype(BF16), w1l=w1[i, :, :, 1::2].astype(BF16),
        b1g=b1[i, :, 0::2].reshape(N_EXPERTS, 1, D_FF), b1l=b1[i, :, 1::2].reshape(N_EXPERTS, 1, D_FF),
        w2=w2[i].astype(BF16), b2=b2[i].reshape(N_EXPERTS, 1, D_MODEL),
        ln2_g=ln2_g[i].reshape(1, D_MODEL), ln2_b=ln2_b[i].reshape(1, D_MODEL))


def _encoder_layer(h, p, dt_bias, a_log, seq_groups, alpha):
    seq_lens = tuple(s for tok0, n, s in seq_groups for _ in range(n))
    u, z, xbc, q, k, v, dt, dtt = _in_proj(h, p['w_all'], p['w_dt_t'])
    pool_out = _pool_mixer(u, p['w_bd'], p['pool_scale'], seq_lens)
    xs, bc = _conv_silu(xbc, p['conv_w'], p['conv_b'], seq_lens)
    y_f, y_b = _ssd_scan(xs, bc, dt, dtt, dt_bias, a_log, seq_lens)
    att = jnp.concatenate([_attention(q, k, v, p['bias_tab'], tok0, n, s) for tok0, n, s in seq_groups], axis=0)
    h1, h1b, idx_t, gate_t = _out_proj_router(h, pool_out, y_f, y_b, xs, z, att, p['w_out'], p['dskip'], p['norm_g'],
                                              p['ln1_g'], p['ln1_b'], p['w_router_t'], p['b_router'], alpha)
    src_tok, gate_slot, blk_e, n_used, dest = _routing(idx_t[:TOP_K].T, gate_t[:TOP_K].T)
    xpad = jnp.take(h1b, src_tok, axis=0)
    ypad = _experts(xpad, gate_slot, blk_e, n_used, p['w1g'], p['w1l'], p['b1g'], p['b1l'], p['w2'], p['b2'])
    y_tok = jnp.take(ypad, dest.T.reshape(-1), axis=0).reshape(TOP_K, h.shape[0], D_MODEL)
    return _combine_ln(h1, y_tok, p['ln2_g'], p['ln2_b'], alpha)


def kernel(x_prompt, x_sample, w_in, conv_w, conv_b, a_log, dt_bias, d_skip, ssd_norm_g, w_pool, pool_scale, rpb, w_out,
           ln1_g, ln1_b, w_router, b_router, w1, b1, w2, b2, ln2_g, ln2_b):
    depth = w_in.shape[0]
    alpha = (2 * depth) ** 0.25
    bp, sp, _ = x_prompt.shape
    bs, ss, _ = x_sample.shape
    seq_groups = ((0, bp, sp), (bp * sp, bs, ss))
    h = jnp.concatenate([x_prompt.reshape(bp * sp, D_MODEL), x_sample.reshape(bs * ss, D_MODEL)], axis=0)
    for i in range(depth):
        p = _prep_layer(i, w_in, conv_w, conv_b, w_pool, pool_scale, rpb, w_out, w_router, b_router, w1, b1, w2, b2,
                        d_skip, ssd_norm_g, ln1_g, ln1_b, ln2_g, ln2_b)
        h = _encoder_layer(h, p, dt_bias[i], a_log[i], seq_groups, alpha)
    y_prompt = h[:bp * sp].reshape(bp, sp, D_MODEL)
    y_sample = h[bp * sp:].reshape(bs, ss, D_MODEL)
    return (y_prompt, y_sample)
```

```python
import functools

import numpy as np
import jax
import jax.numpy as jnp
from jax import lax
from jax.experimental import pallas as pl
from jax.experimental.pallas import tpu as pltpu

F32 = jnp.float32
BF16 = jnp.bfloat16

D_MODEL = 1024
POOL_GROUPS = 4
POOL_GDIM = 64
POOL_W = POOL_GROUPS * POOL_GDIM
POOL_WINDOWS = (2, 4, 8, 16)
SSD_HEADS = 8
SSD_HDIM = 64
SSD_INNER = SSD_HEADS * SSD_HDIM
SSD_GROUPS = 2
SSD_STATE = 128
SSD_CONV = 5
SSD_CHUNK = 128
SSD_BC = 2 * SSD_GROUPS * SSD_STATE
SSD_XBC = SSD_INNER + SSD_BC
SSD_GW = SSD_INNER // SSD_GROUPS
ATT_HEADS = 4
ATT_HDIM = 64
ATT_W = ATT_HEADS * ATT_HDIM
GRID_W = 64
NA_ROWS = 8
NA_COLS = 16
N_EXPERTS = 32
TOP_K = 4
D_FF = D_MODEL
SWIGLU_LIMIT = 7.0
SWIGLU_ALPHA = 1.702
LN_EPS = 1e-5
RMS_EPS = 1e-5

LANES = 128
DT_COLS = 2 * SSD_HEADS
DT_REP = 3
HALO = 8
NEG_BIG = -1e30
NA_KEYS = NA_ROWS * GRID_W

TM_PROJ = 512
TM_OUT = 256
TS_SEQ = 256
ATT_ROWS_PER_STEP = 8
MOE_BM = 512
FF_CHUNK = 512
W1_ROWS = 512
VMEM_LIMIT = 56 * 1024 * 1024


def _cparams(sem):
    return pltpu.CompilerParams(dimension_semantics=sem, vmem_limit_bytes=VMEM_LIMIT)


def _dot(a, b):
    return jnp.dot(a, b, preferred_element_type=F32)


def _dot_nt(a, b, precision=None):
    return lax.dot_general(a, b, (((1,), (1,)), ((), ())), preferred_element_type=F32, precision=precision)


def _softplus(v):
    return jnp.maximum(v, 0.0) + jnp.log1p(jnp.exp(-jnp.abs(v)))


def _split3(v):
    hi = v.astype(BF16)
    r1 = v - hi.astype(F32)
    mid = r1.astype(BF16)
    lo = (r1 - mid.astype(F32)).astype(BF16)
    return hi, mid, lo


def _pack3(v):
    hi, mid, lo = _split3(v)
    lane = lax.broadcasted_iota(jnp.int32, v.shape, 1)
    zero = jnp.zeros(v.shape, BF16)
    return jnp.where(lane < DT_COLS, hi, jnp.where(lane < 2 * DT_COLS, mid, jnp.where(lane < 3 * DT_COLS, lo, zero)))


def _expand_matrix(off, width):
    r = np.arange(LANES)[:, None]
    c = np.arange(SSD_HEADS * width)[None, :]
    m = (r < DT_REP * DT_COLS) & ((r % DT_COLS) == off + c // width)
    return jnp.asarray(m, BF16)


def _inproj_kernel(x_ref, w_ref, wdt_ref, u_ref, z_ref, xbc_ref, q_ref, k_ref, v_ref, dt_ref, dtt_ref):
    xb = x_ref[...].astype(BF16)
    c0 = 0
    for ref, width in ((u_ref, POOL_W), (z_ref, SSD_INNER), (xbc_ref, SSD_XBC), (q_ref, ATT_W),
                       (k_ref, ATT_W), (v_ref, ATT_W), (dt_ref, LANES)):
        ref[...] = _dot(xb, w_ref[:, c0:c0 + width]).astype(ref.dtype)
        c0 += width
    dtt_ref[...] = _dot_nt(wdt_ref[...], xb)


def _in_proj(h, w_all, w_dt_t):
    t = h.shape[0]
    tm = TM_PROJ
    row = lambda w: pl.BlockSpec((tm, w), lambda i: (i, 0))
    full = lambda a: pl.BlockSpec(a.shape, lambda i: (0,) * a.ndim)
    return pl.pallas_call(
        _inproj_kernel,
        grid=(t // tm,),
        in_specs=[row(D_MODEL), full(w_all), full(w_dt_t)],
        out_specs=[row(POOL_W), row(SSD_INNER), row(SSD_XBC), row(ATT_W), row(ATT_W), row(ATT_W), row(LANES),
                   pl.BlockSpec((DT_COLS, tm), lambda i: (0, i))],
        out_shape=[jax.ShapeDtypeStruct((t, POOL_W), F32), jax.ShapeDtypeStruct((t, SSD_INNER), F32),
                   jax.ShapeDtypeStruct((t, SSD_XBC), F32), jax.ShapeDtypeStruct((t, ATT_W), BF16),
                   jax.ShapeDtypeStruct((t, ATT_W), BF16), jax.ShapeDtypeStruct((t, ATT_W), BF16),
                   jax.ShapeDtypeStruct((t, LANES), F32), jax.ShapeDtypeStruct((DT_COLS, t), F32)],
        compiler_params=_cparams(("parallel",)),
        name="in_proj",
    )(h, w_all, w_dt_t)


def _seq_tables(seq_lens, tile):
    pos, length = [], []
    for s in seq_lens:
        assert s % tile == 0
        for p in range(0, s, tile):
            pos.append(p)
            length.append(s)
    return np.asarray(pos, np.int32), np.asarray(length, np.int32)


def _halo_specs(ts, width, n_tok):
    hb = ts // HALO
    last = n_tok // HALO - 1
    cur = pl.BlockSpec((ts, width), lambda i, *_: (i, 0))
    prev = pl.BlockSpec((HALO, width), lambda i, *_: (jnp.maximum(i * hb - 1, 0), 0))
    nxt = pl.BlockSpec((HALO, width), lambda i, *_: (jnp.minimum((i + 1) * hb, last), 0))
    return prev, cur, nxt


def _extended(prev_ref, cur_ref, next_ref, pos0, slen, ts):
    prev = jnp.where(pos0 > 0, prev_ref[...], 0.0)
    nxt = jnp.where(pos0 + ts < slen, next_ref[...], 0.0)
    return jnp.concatenate([prev, cur_ref[...], nxt], axis=0)


def _rows_from(ext, k, ts):
    n = ext.shape[0]
    shift = (n - (HALO + k)) % n
    rolled = ext if shift == 0 else pltpu.roll(ext, shift, 0)
    return rolled[:ts]


def _pool_kernel(pos_ref, len_ref, prev_ref, cur_ref, next_ref, w_ref, scale_ref, o_ref, *, ts):
    i = pl.program_id(0)
    pos0 = pos_ref[i]
    slen = len_ref[i]
    ext = _extended(prev_ref, cur_ref, next_ref, pos0, slen, ts)
    n = ext.shape[0]

    def pair_sum(a, k):
        return a + pltpu.roll(a, n - k, 0)

    a2 = pair_sum(ext, 1)
    a4 = pair_sum(a2, 2)
    a8 = pair_sum(a4, 4)
    a16 = pair_sum(a8, 8)
    sums = [_rows_from(a, -w // 2, ts) for a, w in zip((a2, a4, a8, a16), POOL_WINDOWS)]

    lane = lax.broadcasted_iota(jnp.int32, (ts, POOL_W), 1)
    grp = lane // POOL_GDIM
    wsum = jnp.where(grp == 0, sums[0], jnp.where(grp == 1, sums[1], jnp.where(grp == 2, sums[2], sums[3])))
    half = jnp.where(grp == 0, 1, jnp.where(grp == 1, 2, jnp.where(grp == 2, 4, 8)))
    tpos = pos0 + lax.broadcasted_iota(jnp.int32, (ts, POOL_W), 0)
    cnt = (jnp.minimum(tpos + half, slen) - jnp.maximum(tpos - half, 0)).astype(F32)
    pooled = wsum / cnt - cur_ref[...]
    o_ref[...] = _dot(pooled.astype(BF16), w_ref[...]) * scale_ref[...]


def _pool_mixer(u, w_bd, scale, seq_lens):
    t = u.shape[0]
    ts = TS_SEQ
    pos, length = _seq_tables(seq_lens, ts)
    prev, cur, nxt = _halo_specs(ts, POOL_W, t)
    grid_spec = pltpu.PrefetchScalarGridSpec(
        num_scalar_prefetch=2, grid=(t // ts,),
        in_specs=[prev, cur, nxt,
                  pl.BlockSpec((POOL_W, POOL_W), lambda i, *_: (0, 0)),
                  pl.BlockSpec((1, POOL_W), lambda i, *_: (0, 0))],
        out_specs=pl.BlockSpec((ts, POOL_W), lambda i, *_: (i, 0)))
    return pl.pallas_call(
        functools.partial(_pool_kernel, ts=ts), grid_spec=grid_spec,
        out_shape=jax.ShapeDtypeStruct((t, POOL_W), F32),
        compiler_params=_cparams(("parallel",)), name="pool_mixer",
    )(pos, length, u, u, u, w_bd, scale)


def _conv_kernel(pos_ref, len_ref, prev_ref, cur_ref, next_ref, dt_ref, w_ref, b_ref, dtb_ref, ef_ref, eb_ref,
                 xs_ref, xcf_ref, xcb_ref, c_ref, bt_ref, *, ts):
    i = pl.program_id(0)
    ext = _extended(prev_ref, cur_ref, next_ref, pos_ref[i], len_ref[i], ts)
    acc = jnp.zeros((ts, SSD_XBC), F32) + b_ref[...]
    for j in range(SSD_CONV):
        acc = acc + _rows_from(ext, j - SSD_CONV // 2, ts) * w_ref[j:j + 1, :]
    act = acc * jax.nn.sigmoid(acc)
    xs = act[:, :SSD_INNER]
    xs_ref[...] = xs
    dt_pk = _pack3(_softplus(dt_ref[...] + dtb_ref[...]))
    xcf_ref[...] = (xs * _dot(dt_pk, ef_ref[...])).astype(BF16)
    xcb_ref[...] = (xs * _dot(dt_pk, eb_ref[...])).astype(BF16)
    nb = SSD_GROUPS * SSD_STATE
    bt_ref[...] = act[:, SSD_INNER:SSD_INNER + nb].T.astype(BF16)
    c_ref[...] = act[:, SSD_INNER + nb:].astype(BF16)


def _conv_silu(xbc, dt, conv_w, conv_b, dt_bias_row, seq_lens):
    t = xbc.shape[0]
    ts = TS_SEQ
    pos, length = _seq_tables(seq_lens, ts)
    prev, cur, nxt = _halo_specs(ts, SSD_XBC, t)
    e_f = _expand_matrix(0, SSD_HDIM)
    e_b = _expand_matrix(SSD_HEADS, SSD_HDIM)
    nb = SSD_GROUPS * SSD_STATE
    const = lambda a: pl.BlockSpec(a.shape, lambda i, *_: (0, 0))
    row = lambda w: pl.BlockSpec((ts, w), lambda i, *_: (i, 0))
    grid_spec = pltpu.PrefetchScalarGridSpec(
        num_scalar_prefetch=2, grid=(t // ts,),
        in_specs=[prev, cur, nxt, row(LANES), const(conv_w), const(conv_b), const(dt_bias_row), const(e_f), const(e_b)],
        out_specs=[row(SSD_INNER), row(SSD_INNER), row(SSD_INNER), row(nb),
                   pl.BlockSpec((nb, ts), lambda i, *_: (0, i))])
    return pl.pallas_call(
        functools.partial(_conv_kernel, ts=ts), grid_spec=grid_spec,
        out_shape=[jax.ShapeDtypeStruct((t, SSD_INNER), F32), jax.ShapeDtypeStruct((t, SSD_INNER), BF16),
                   jax.ShapeDtypeStruct((t, SSD_INNER), BF16), jax.ShapeDtypeStruct((t, nb), BF16),
                   jax.ShapeDtypeStruct((nb, t), BF16)],
        compiler_params=_cparams(("parallel",)), name="conv_silu",
    )(pos, length, xbc, xbc, xbc, dt, conv_w, conv_b, dt_bias_row, e_f, e_b)


def _ssd_chunk(xc, c, bt, dt_full, dtt, aneg_full, aneg_col, e128, state_ref, reset, reverse, off):
    L = SSD_CHUNK
    ii = lax.broadcasted_iota(jnp.int32, (L, L), 0)
    jj = lax.broadcasted_iota(jnp.int32, (L, L), 1)
    tri = (jj >= ii) if reverse else (jj <= ii)
    tri_t = (ii >= jj) if reverse else (ii <= jj)
    tri_b = tri.astype(BF16)
    tri_tb = tri_t.astype(BF16)
    acs = sum(_dot(tri_b, part) for part in _split3(dt_full * aneg_full))
    acs_row = sum(_dot(part, tri_tb) for part in _split3(dtt * aneg_col))
    edge = 0 if reverse else L - 1
    acs_b = _dot(_pack3(acs), e128)
    lane = lax.broadcasted_iota(jnp.int32, (L, LANES), 1)
    low = lane < SSD_HDIM
    acs_e = jnp.concatenate(
        [jnp.where(low, acs_b[:, (2 * j) * LANES:(2 * j + 1) * LANES], acs_b[:, (2 * j + 1) * LANES:(2 * j + 2) * LANES])
         for j in range(SSD_HEADS // 2)], axis=1)
    tot_e = acs_e[edge:edge + 1, :]
    exp_acs = jnp.exp(acs_e)
    to_end = jnp.exp(tot_e - acs_e)
    chunk_decay = jnp.exp(tot_e)
    xw = (xc.astype(F32) * to_end).astype(BF16)
    zero_b = jnp.zeros((L, LANES), BF16)

    ys = []
    for g in range(SSD_GROUPS):
        gs = slice(g * SSD_GW, (g + 1) * SSD_GW)
        c_g = c[:, g * SSD_STATE:(g + 1) * SSD_STATE]
        bt_g = bt[g * SSD_STATE:(g + 1) * SSD_STATE, :]
        cb = _dot(c_g, bt_g)
        prev = jnp.where(reset, 0.0, state_ref[g])
        y_off = _dot(c_g, prev.astype(BF16)) * exp_acs[:, gs]
        state_ref[g] = prev * chunk_decay[:, gs] + _dot(bt_g, xw[:, gs])
        for pair in range(SSD_GW // LANES):
            t0 = g * SSD_GW + pair * LANES
            tile = xc[:, t0:t0 + LANES]
            halves = (jnp.where(low, tile, zero_b), jnp.where(low, zero_b, tile))
            y_pair = y_off[:, pair * LANES:(pair + 1) * LANES]
            for s in range(2):
                h = t0 // SSD_HDIM + s
                seg = acs_b[:, h * LANES:(h + 1) * LANES] - acs_row[off + h:off + h + 1, :]
                m = (cb * jnp.exp(jnp.where(tri, seg, NEG_BIG))).astype(BF16)
                y_pair = y_pair + _dot(m, halves[s])
            ys.append(y_pair)
    return jnp.concatenate(ys, axis=1)


def _ssd_kernel(mir_ref, start_ref, xc_f, c_f, bt_f, dt_f, dtt_f, xc_b, c_b, bt_b, dt_b, dtt_b,
                bias_row, bias_col, alog_row, alog_col, ef_ref, eb_ref, yf_ref, yb_ref, st_f, st_b):
    step = pl.program_id(0)
    reset = start_ref[step] == 1
    aneg_full = -jnp.exp(alog_row[...])
    aneg_col = -jnp.exp(alog_col[...])
    dtc_f = _softplus(dt_f[...] + bias_row[...])
    dtr_f = _softplus(dtt_f[...] + bias_col[...])
    dtc_b = _softplus(dt_b[...] + bias_row[...])
    dtr_b = _softplus(dtt_b[...] + bias_col[...])
    yf_ref[...] = _ssd_chunk(xc_f[...], c_f[...], bt_f[...], dtc_f, dtr_f, aneg_full, aneg_col, ef_ref[...],
                             st_f, reset, False, 0)
    yb_ref[...] = _ssd_chunk(xc_b[...], c_b[...], bt_b[...], dtc_b, dtr_b, aneg_full, aneg_col, eb_ref[...],
                             st_b, reset, True, SSD_HEADS)


def _ssd_tables(seq_lens):
    mirror, start = [], []
    c0 = 0
    for s in seq_lens:
        assert s % SSD_CHUNK == 0
        n = s // SSD_CHUNK
        for c in range(n):
            mirror.append(c0 + n - 1 - c)
            start.append(1 if c == 0 else 0)
        c0 += n
    return np.asarray(mirror, np.int32), np.asarray(start, np.int32)


def _ssd_scan(xcf, xcb, c, bt, dt, dtt, dt_bias_row, dt_bias_col, alog_row, alog_col, seq_lens):
    t = xcf.shape[0]
    L = SSD_CHUNK
    nb = SSD_GROUPS * SSD_STATE
    mirror, start = _ssd_tables(seq_lens)
    e_f = _expand_matrix(0, LANES)
    e_b = _expand_matrix(SSD_HEADS, LANES)
    fw = lambda w: pl.BlockSpec((L, w), lambda s, m, st: (s, 0))
    bw = lambda w: pl.BlockSpec((L, w), lambda s, m, st: (m[s], 0))
    fw_t = lambda r: pl.BlockSpec((r, L), lambda s, m, st: (0, s))
    bw_t = lambda r: pl.BlockSpec((r, L), lambda s, m, st: (0, m[s]))
    const = lambda a: pl.BlockSpec(a.shape, lambda s, m, st: (0, 0))
    consts = (dt_bias_row, dt_bias_col, alog_row, alog_col, e_f, e_b)
    grid_spec = pltpu.PrefetchScalarGridSpec(
        num_scalar_prefetch=2, grid=(t // L,),
        in_specs=[fw(SSD_INNER), fw(nb), fw_t(nb), fw(LANES), fw_t(DT_COLS),
                  bw(SSD_INNER), bw(nb), bw_t(nb), bw(LANES), bw_t(DT_COLS)] + [const(a) for a in consts],
        out_specs=[fw(SSD_INNER), bw(SSD_INNER)],
        scratch_shapes=[pltpu.VMEM((SSD_GROUPS, SSD_STATE, SSD_GW), F32),
                        pltpu.VMEM((SSD_GROUPS, SSD_STATE, SSD_GW), F32)])
    return pl.pallas_call(
        _ssd_kernel, grid_spec=grid_spec,
        out_shape=[jax.ShapeDtypeStruct((t, SSD_INNER), F32), jax.ShapeDtypeStruct((t, SSD_INNER), F32)],
        compiler_params=_cparams(("arbitrary",)), name="ssd_scan",
    )(mirror, start, xcf, c, bt, dt, dtt, xcb, c, bt, dt, dtt, *consts)


def _att_bias_table(rpb):
    v = np.arange(NA_ROWS)[:, None]
    i = np.arange(NA_ROWS)[None, :]
    dr = i - v + NA_ROWS - 1
    c = np.arange(GRID_W)[:, None]
    kc = np.arange(GRID_W)[None, :]
    cs = np.clip(c - NA_COLS // 2, 0, GRID_W - NA_COLS)
    valid = (kc >= cs) & (kc < cs + NA_COLS)
    dc = np.clip(kc - c + NA_COLS - 1, 0, 2 * NA_COLS - 2)
    tab = rpb[:, dr[:, :, None, None], dc[None, None, :, :]]
    tab = jnp.where(valid[None, None, None], tab.astype(F32), NEG_BIG)
    tab = jnp.transpose(tab, (1, 0, 3, 2, 4))
    return tab.reshape(NA_ROWS, ATT_HEADS, GRID_W, NA_KEYS)


def _att_kernel(q_ref, k_ref, v_ref, bias_ref, o_ref, *, rows, rb):
    j = pl.program_id(1)
    scale = ATT_HDIM ** -0.5
    for rr in range(rb):
        r = j * rb + rr
        rs = jnp.clip(r - NA_ROWS // 2, 0, rows - NA_ROWS)
        voff = r - rs
        k0 = pl.multiple_of(rs * GRID_W, GRID_W)
        kwin = k_ref[pl.ds(k0, NA_KEYS), :]
        vwin = v_ref[pl.ds(k0, NA_KEYS), :]
        q = q_ref[rr * GRID_W:(rr + 1) * GRID_W, :]
        outs = []
        for h in range(ATT_HEADS):
            sl = slice(h * ATT_HDIM, (h + 1) * ATT_HDIM)
            s = _dot_nt(q[:, sl], kwin[:, sl]) * scale + bias_ref[voff, h]
            m = jnp.max(s, axis=-1, keepdims=True)
            p = jnp.exp(s - m)
            den = jnp.sum(p, axis=-1, keepdims=True)
            outs.append(_dot(p.astype(BF16), vwin[:, sl]) / den)
        o_ref[rr * GRID_W:(rr + 1) * GRID_W, :] = jnp.concatenate(outs, axis=1).astype(o_ref.dtype)


def _attention(q, k, v, bias_tab, tok0, n_seq, seq_len):
    rows = seq_len // GRID_W
    assert rows >= NA_ROWS and tok0 % seq_len == 0
    rb = ATT_ROWS_PER_STEP
    steps = rows // rb
    s0 = tok0 // seq_len
    qb0 = tok0 // (rb * GRID_W)
    return pl.pallas_call(
        functools.partial(_att_kernel, rows=rows, rb=rb),
        grid=(n_seq, steps),
        in_specs=[pl.BlockSpec((rb * GRID_W, ATT_W), lambda b, j: (qb0 + b * steps + j, 0)),
                  pl.BlockSpec((seq_len, ATT_W), lambda b, j: (s0 + b, 0)),
                  pl.BlockSpec((seq_len, ATT_W), lambda b, j: (s0 + b, 0)),
                  pl.BlockSpec(bias_tab.shape, lambda b, j: (0, 0, 0, 0))],
        out_specs=pl.BlockSpec((rb * GRID_W, ATT_W), lambda b, j: (b * steps + j, 0)),
        out_shape=jax.ShapeDtypeStruct((n_seq * seq_len, ATT_W), BF16),
        compiler_params=_cparams(("parallel", "parallel")), name="nbr_attention",
    )(q, k, v, bias_tab)


def _layer_norm(x, g, b):
    mu = jnp.mean(x, axis=-1, keepdims=True)
    xc = x - mu
    var = jnp.mean(xc * xc, axis=-1, keepdims=True)
    return xc * lax.rsqrt(var + LN_EPS) * g + b


def _outproj_kernel(h_ref, pool_ref, yf_ref, yb_ref, xs_ref, z_ref, att_ref, wo_ref, dskip_ref, ng_ref,
                    g_ref, b_ref, wr_ref, br_ref, h1_ref, h1b_ref, idx_ref, gate_ref, *, alpha):
    y = yf_ref[...] + yb_ref[...] + xs_ref[...] * dskip_ref[...]
    z = z_ref[...]
    y = y * (z * jax.nn.sigmoid(z))
    parts = []
    for g in range(SSD_GROUPS):
        yg = y[:, g * SSD_GW:(g + 1) * SSD_GW]
        parts.append(yg * lax.rsqrt(jnp.mean(yg * yg, axis=-1, keepdims=True) + RMS_EPS))
    ssd = (jnp.concatenate(parts, axis=1) * ng_ref[...]).astype(BF16)
    m = (_dot(pool_ref[...].astype(BF16), wo_ref[0:POOL_W, :])
         + _dot(ssd, wo_ref[POOL_W:POOL_W + SSD_INNER, :])
         + _dot(att_ref[...], wo_ref[POOL_W + SSD_INNER:, :]))
    h1 = _layer_norm(alpha * h_ref[...] + m, g_ref[...], b_ref[...])
    h1_ref[...] = h1
    h1b_ref[...] = h1.astype(BF16)

    logits = _dot_nt(wr_ref[...], h1, precision=jax.lax.Precision.HIGHEST) + br_ref[...]
    eidx = lax.broadcasted_iota(jnp.int32, logits.shape, 0)
    vals, idxs = [], []
    cur = logits
    for _ in range(TOP_K):
        mx = jnp.max(cur, axis=0, keepdims=True)
        ix = jnp.min(jnp.where(cur == mx, eidx, N_EXPERTS), axis=0, keepdims=True)
        vals.append(mx)
        idxs.append(ix)
        cur = jnp.where(eidx == ix, -jnp.inf, cur)
    es = [jnp.exp(vv - vals[0]) for vv in vals]
    den = es[0] + es[1] + es[2] + es[3]
    tm = logits.shape[1]
    idx_ref[...] = jnp.concatenate(idxs + [jnp.zeros((8 - TOP_K, tm), jnp.int32)], axis=0)
    gates_t = jnp.concatenate([e / den for e in es] + [jnp.zeros((LANES - TOP_K, tm), F32)], axis=0)
    gate_ref[...] = gates_t.T


def _out_proj_router(h, pool_out, y_f, y_b, xs, z, att, w_out, dskip, norm_g, ln_g, ln_b, w_router_t, b_router, alpha):
    t = h.shape[0]
    tm = TM_OUT
    row = lambda w: pl.BlockSpec((tm, w), lambda i: (i, 0))
    full = lambda a: pl.BlockSpec(a.shape, lambda i: (0,) * a.ndim)
    consts = (w_out, dskip, norm_g, ln_g, ln_b, w_router_t, b_router)
    return pl.pallas_call(
        functools.partial(_outproj_kernel, alpha=alpha),
        grid=(t // tm,),
        in_specs=[row(D_MODEL), row(POOL_W), row(SSD_INNER), row(SSD_INNER), row(SSD_INNER), row(SSD_INNER),
                  row(ATT_W)] + [full(a) for a in consts],
        out_specs=[row(D_MODEL), row(D_MODEL), pl.BlockSpec((8, tm), lambda i: (0, i)), row(LANES)],
        out_shape=[jax.ShapeDtypeStruct((t, D_MODEL), F32), jax.ShapeDtypeStruct((t, D_MODEL), BF16),
                   jax.ShapeDtypeStruct((8, t), jnp.int32), jax.ShapeDtypeStruct((t, LANES), F32)],
        compiler_params=_cparams(("parallel",)), name="out_proj_router",
    )(h, pool_out, y_f, y_b, xs, z, att, *consts)


def _w1_prep_kernel(w_ref, perm_ref, g_ref, l_ref):
    for j in range(D_FF // LANES):
        chunk = w_ref[:, 2 * LANES * j:2 * LANES * (j + 1)].astype(BF16)
        r = _dot(chunk, perm_ref[...])
        g_ref[:, LANES * j:LANES * (j + 1)] = r[:, :LANES].astype(BF16)
        l_ref[:, LANES * j:LANES * (j + 1)] = r[:, LANES:].astype(BF16)


def _w1_prep(w1):
    n_e = w1.shape[0]
    o = np.arange(2 * LANES)[None, :]
    c = np.arange(2 * LANES)[:, None]
    perm = jnp.asarray(np.where(o < LANES, c == 2 * o, c == 2 * (o - LANES) + 1), BF16)
    rt = W1_ROWS
    out = pl.BlockSpec((None, rt, D_FF), lambda e, r: (e, r, 0))
    return pl.pallas_call(
        _w1_prep_kernel, grid=(n_e, D_MODEL // rt),
        in_specs=[pl.BlockSpec((None, rt, 2 * D_FF), lambda e, r: (e, r, 0)),
                  pl.BlockSpec(perm.shape, lambda e, r: (0, 0))],
        out_specs=[out, out],
        out_shape=[jax.ShapeDtypeStruct((n_e, D_MODEL, D_FF), BF16)] * 2,
        compiler_params=_cparams(("parallel", "parallel")), name="w1_relayout",
    )(w1, perm)


def _expert_kernel(blk_ref, e_ref, lo_ref, hi_ref, first_ref, x_ref, w1g_ref, w1l_ref, b1g_ref, b1l_ref,
                   w2_ref, b2_ref, o_ref):
    w = pl.program_id(0)
    lo = lo_ref[w]
    hi = hi_ref[w]

    @pl.when(hi > lo)
    def _():
        x = x_ref[...]
        acc = jnp.zeros(o_ref.shape, F32) + b2_ref[...]
        for c in range(0, D_FF, FF_CHUNK):
            hg = _dot(x, w1g_ref[:, c:c + FF_CHUNK]) + b1g_ref[:, c:c + FF_CHUNK]
            hl = _dot(x, w1l_ref[:, c:c + FF_CHUNK]) + b1l_ref[:, c:c + FF_CHUNK]
            hg = jnp.minimum(hg, SWIGLU_LIMIT)
            hl = jnp.clip(hl, -SWIGLU_LIMIT, SWIGLU_LIMIT)
            act = hg * jax.nn.sigmoid(SWIGLU_ALPHA * hg) * (hl + 1.0)
            acc = acc + _dot(act.astype(BF16), w2_ref[c:c + FF_CHUNK, :])
        rows = blk_ref[w] * MOE_BM + lax.broadcasted_iota(jnp.int32, (MOE_BM, 1), 0)
        mine = (rows >= lo) & (rows < hi)
        y = acc.astype(o_ref.dtype)

        @pl.when(first_ref[w] == 1)
        def _():
            o_ref[...] = jnp.where(mine, y, jnp.zeros_like(y))

        @pl.when(first_ref[w] == 0)
        def _():
            o_ref[...] = jnp.where(mine, y, o_ref[...])


def _experts(x_sorted, meta, w1g, w1l, b1g, b1l, w2, b2):
    n_rows = x_sorted.shape[0]
    bm = MOE_BM
    n_items = meta[0].shape[0]
    wspec = lambda r, c: pl.BlockSpec((None, r, c), lambda w, blk, e, *_: (e[w], 0, 0))
    grid_spec = pltpu.PrefetchScalarGridSpec(
        num_scalar_prefetch=5, grid=(n_items,),
        in_specs=[pl.BlockSpec((bm, D_MODEL), lambda w, blk, *_: (blk[w], 0)),
                  wspec(D_MODEL, D_FF), wspec(D_MODEL, D_FF), wspec(1, D_FF), wspec(1, D_FF),
                  wspec(D_FF, D_MODEL), wspec(1, D_MODEL)],
        out_specs=pl.BlockSpec((bm, D_MODEL), lambda w, blk, *_: (blk[w], 0)))
    return pl.pallas_call(
        _expert_kernel, grid_spec=grid_spec,
        out_shape=jax.ShapeDtypeStruct((n_rows, D_MODEL), BF16),
        compiler_params=_cparams(("arbitrary",)), name="moe_experts",
    )(*meta, x_sorted, w1g, w1l, b1g, b1l, w2, b2)


def _combine_kernel(h_ref, y_ref, gate_ref, g_ref, b_ref, o_ref, *, alpha):
    gates = gate_ref[...]
    f = y_ref[0].astype(F32) * gates[:, 0:1]
    for k in range(1, TOP_K):
        f = f + y_ref[k].astype(F32) * gates[:, k:k + 1]
    o_ref[...] = _layer_norm(alpha * h_ref[...] + f, g_ref[...], b_ref[...])


def _combine_ln(h1, y_tok, gate_col, ln_g, ln_b, alpha):
    t = h1.shape[0]
    tm = TM_OUT
    return pl.pallas_call(
        functools.partial(_combine_kernel, alpha=alpha),
        grid=(t // tm,),
        in_specs=[pl.BlockSpec((tm, D_MODEL), lambda i: (i, 0)),
                  pl.BlockSpec((TOP_K, tm, D_MODEL), lambda i: (0, i, 0)),
                  pl.BlockSpec((tm, LANES), lambda i: (i, 0)),
                  pl.BlockSpec((1, D_MODEL), lambda i: (0, 0)),
                  pl.BlockSpec((1, D_MODEL), lambda i: (0, 0))],
        out_specs=pl.BlockSpec((tm, D_MODEL), lambda i: (i, 0)),
        out_shape=jax.ShapeDtypeStruct((t, D_MODEL), F32),
        compiler_params=_cparams(("parallel",)), name="combine_ln",
    )(h1, y_tok, gate_col, ln_g, ln_b)


def _routing(idx_t):
    t = idx_t.shape[1]
    n_asg = TOP_K * t
    bm = MOE_BM
    n_blocks = n_asg // bm
    n_items = n_blocks + N_EXPERTS - 1
    i32 = jnp.int32
    flat_e = idx_t[:TOP_K].reshape(-1)
    a = jnp.arange(n_asg, dtype=i32)
    skeys, order = lax.sort((flat_e * n_asg + a, a), num_keys=1)
    _, pos = lax.sort((order, a), num_keys=1)
    src_tok = order % t
    bounds = jnp.searchsorted(skeys, jnp.arange(N_EXPERTS + 1, dtype=i32) * n_asg).astype(i32)
    starts, ends = bounds[:-1], bounds[1:]
    first_tile = starts // bm
    ntiles = jnp.where(ends > starts, (ends + bm - 1) // bm - first_tile, 0)
    cum = jnp.cumsum(ntiles)
    base = cum - ntiles
    total = cum[-1]
    w = jnp.arange(n_items, dtype=i32)
    valid = w < total
    e_w = jnp.minimum(jnp.searchsorted(cum, w, side='right'), N_EXPERTS - 1).astype(i32)
    e_last = e_w[jnp.maximum(total - 1, 0)]
    e_w = jnp.where(valid, e_w, e_last)
    blk = jnp.where(valid, first_tile[e_w] + (w - base[e_w]), n_blocks - 1).astype(i32)
    lo = jnp.where(valid, jnp.maximum(starts[e_w], blk * bm), 0).astype(i32)
    hi = jnp.where(valid, jnp.minimum(ends[e_w], (blk + 1) * bm), 0).astype(i32)
    first = jnp.concatenate([jnp.ones((1,), i32), (blk[1:] != blk[:-1]).astype(i32)])
    return src_tok, pos, (blk, e_w, lo, hi, first)


def _prep_layer(i, w_in, conv_w, conv_b, a_log, dt_bias, w_pool, pool_scale, rpb, w_out, w_router, b_router,
                w1, b1, w2, b2, d_skip, ssd_norm_g, ln1_g, ln1_b, ln2_g, ln2_b):
    c_x = POOL_W + SSD_INNER + SSD_XBC
    c_dt = c_x + DT_COLS
    w = w_in[i]
    w_dt = w[:, c_x:c_dt]
    pad = LANES - DT_REP * DT_COLS
    w_all = jnp.concatenate([w[:, :c_x], w[:, c_dt:]] + [w_dt] * DT_REP + [jnp.zeros((D_MODEL, pad), F32)],
                            axis=1).astype(BF16)
    packed_row = lambda v: jnp.concatenate([v.reshape(1, DT_COLS)] * DT_REP + [jnp.zeros((1, pad), F32)], axis=1)
    w_bd = jnp.zeros((POOL_W, POOL_W), F32)
    for g in range(POOL_GROUPS):
        sl = slice(g * POOL_GDIM, (g + 1) * POOL_GDIM)
        w_bd = w_bd.at[sl, sl].set(w_pool[i, g])
    conv_w_pad = jnp.concatenate([conv_w[i], jnp.zeros((HALO - SSD_CONV, SSD_XBC), F32)], axis=0)
    w1g, w1l = _w1_prep(w1[i])
    return dict(
        w_all=w_all, w_dt_t=w_dt.T.astype(BF16), w_bd=w_bd.astype(BF16), pool_scale=pool_scale[i].reshape(1, POOL_W),
        conv_w=conv_w_pad, conv_b=conv_b[i].reshape(1, SSD_XBC),
        dt_bias_row=packed_row(dt_bias[i]), dt_bias_col=dt_bias[i].reshape(DT_COLS, 1),
        alog_row=packed_row(a_log[i]), alog_col=a_log[i].reshape(DT_COLS, 1),
        bias_tab=_att_bias_table(rpb[i]),
        w_out=w_out[i].astype(BF16), dskip=jnp.repeat(d_skip[i], SSD_HDIM).reshape(1, SSD_INNER),
        norm_g=ssd_norm_g[i].reshape(1, SSD_INNER), ln1_g=ln1_g[i].reshape(1, D_MODEL), ln1_b=ln1_b[i].reshape(1, D_MODEL),
        w_router_t=w_router[i].T, b_router=b_router[i].reshape(N_EXPERTS, 1),
        w1g=w1g, w1l=w1l,
        b1g=b1[i, :, 0::2].reshape(N_EXPERTS, 1, D_FF), b1l=b1[i, :, 1::2].reshape(N_EXPERTS, 1, D_FF),
        w2=w2[i].astype(BF16), b2=b2[i].reshape(N_EXPERTS, 1, D_MODEL),
        ln2_g=ln2_g[i].reshape(1, D_MODEL), ln2_b=ln2_b[i].reshape(1, D_MODEL))


def _encoder_layer(h, p, seq_groups, alpha):
    t = h.shape[0]
    seq_lens = tuple(s for tok0, n, s in seq_groups for _ in range(n))
    u, z, xbc, q, k, v, dt, dtt = _in_proj(h, p['w_all'], p['w_dt_t'])
    pool_out = _pool_mixer(u, p['w_bd'], p['pool_scale'], seq_lens)
    xs, xcf, xcb, c, bt = _conv_silu(xbc, dt, p['conv_w'], p['conv_b'], p['dt_bias_row'], seq_lens)
    y_f, y_b = _ssd_scan(xcf, xcb, c, bt, dt, dtt, p['dt_bias_row'], p['dt_bias_col'], p['alog_row'], p['alog_col'],
                         seq_lens)
    att = jnp.concatenate([_attention(q, k, v, p['bias_tab'], tok0, n, s) for tok0, n, s in seq_groups], axis=0)
    h1, h1b, idx_t, gate_col = _out_proj_router(h, pool_out, y_f, y_b, xs, z, att, p['w_out'], p['dskip'], p['norm_g'],
                                                p['ln1_g'], p['ln1_b'], p['w_router_t'], p['b_router'], alpha)
    src_tok, pos, meta = _routing(idx_t)
    x_sorted = jnp.take(h1b, src_tok, axis=0)
    y_sorted = _experts(x_sorted, meta, p['w1g'], p['w1l'], p['b1g'], p['b1l'], p['w2'], p['b2'])
    y_tok = jnp.take(y_sorted, pos, axis=0).reshape(TOP_K, t, D_MODEL)
    return _combine_ln(h1, y_tok, gate_col, p['ln2_g'], p['ln2_b'], alpha)


def kernel(x_prompt, x_sample, w_in, conv_w, conv_b, a_log, dt_bias, d_skip, ssd_norm_g, w_pool, pool_scale, rpb, w_out,
           ln1_g, ln1_b, w_router, b_router, w1, b1, w2, b2, ln2_g, ln2_b):
    depth = w_in.shape[0]
    alpha = (2 * depth) ** 0.25
    bp, sp, _ = x_prompt.shape
    bs, ss, _ = x_sample.shape
    seq_groups = ((0, bp, sp), (bp * sp, bs, ss))
    h = jnp.concatenate([x_prompt.reshape(bp * sp, D_MODEL), x_sample.reshape(bs * ss, D_MODEL)], axis=0)
    for i in range(depth):
        p = _prep_layer(i, w_in, conv_w, conv_b, a_log, dt_bias, w_pool, pool_scale, rpb, w_out, w_router, b_router,
                        w1, b1, w2, b2, d_skip, ssd_norm_g, ln1_g, ln1_b, ln2_g, ln2_b)
        h = _encoder_layer(h, p, seq_groups, alpha)
    y_prompt = h[:bp * sp].reshape(bp, sp, D_MODEL)
    y_sample = h[bp * sp:].reshape(bs, ss, D_MODEL)
    return (y_prompt, y_sample)
```

```python
import functools

import numpy as np
import jax
import jax.numpy as jnp
from jax import lax
from jax.experimental import pallas as pl
from jax.experimental.pallas import tpu as pltpu

F32 = jnp.float32
BF16 = jnp.bfloat16

D_MODEL = 1024
POOL_GROUPS = 4
POOL_GDIM = 64
POOL_W = POOL_GROUPS * POOL_GDIM
POOL_WINDOWS = (2, 4, 8, 16)
SSD_HEADS = 8
SSD_HDIM = 64
SSD_INNER = SSD_HEADS * SSD_HDIM
SSD_GROUPS = 2
SSD_STATE = 128
SSD_CONV = 5
SSD_CHUNK = 128
SSD_BC = 2 * SSD_GROUPS * SSD_STATE
SSD_XBC = SSD_INNER + SSD_BC
SSD_GW = SSD_INNER // SSD_GROUPS
ATT_HEADS = 4
ATT_HDIM = 64
ATT_W = ATT_HEADS * ATT_HDIM
GRID_W = 64
NA_ROWS = 8
NA_COLS = 16
N_EXPERTS = 32
TOP_K = 4
D_FF = D_MODEL
SWIGLU_LIMIT = 7.0
SWIGLU_ALPHA = 1.702
LN_EPS = 1e-5
RMS_EPS = 1e-5

LANES = 128
DT_COLS = 2 * SSD_HEADS
DT_REP = 3
HALO = 8
NEG_BIG = -1e30
WIN_ROWS = NA_ROWS + 2
WIN_KEYS = WIN_ROWS * GRID_W

TM_PROJ = 512
TM_OUT = 256
TS_SEQ = 256
ATT_PAIRS_PER_STEP = 4
MOE_BM = 512
FF_CHUNK = 512
VMEM_LIMIT = 56 * 1024 * 1024


def _cparams(sem):
    return pltpu.CompilerParams(dimension_semantics=sem, vmem_limit_bytes=VMEM_LIMIT)


def _dot(a, b):
    return jnp.dot(a, b, preferred_element_type=F32)


def _dot_nt(a, b, precision=None):
    return lax.dot_general(a, b, (((1,), (1,)), ((), ())), preferred_element_type=F32, precision=precision)


def _softplus(v):
    return jnp.maximum(v, 0.0) + jnp.log1p(jnp.exp(-jnp.abs(v)))


def _split3(v):
    hi = v.astype(BF16)
    r1 = v - hi.astype(F32)
    mid = r1.astype(BF16)
    lo = (r1 - mid.astype(F32)).astype(BF16)
    return hi, mid, lo


def _pack3(v):
    hi, mid, lo = _split3(v)
    lane = lax.broadcasted_iota(jnp.int32, v.shape, 1)
    zero = jnp.zeros(v.shape, BF16)
    return jnp.where(lane < DT_COLS, hi, jnp.where(lane < 2 * DT_COLS, mid, jnp.where(lane < 3 * DT_COLS, lo, zero)))


def _expand_matrix(off, width):
    r = np.arange(LANES)[:, None]
    c = np.arange(SSD_HEADS * width)[None, :]
    m = (r < DT_REP * DT_COLS) & ((r % DT_COLS) == off + c // width)
    return jnp.asarray(m, BF16)


def _inproj_kernel(x_ref, w_ref, wt_ref, u_ref, z_ref, xbc_ref, k_ref, dt_ref, qt_ref, vt_ref, dtt_ref):
    xb = x_ref[...].astype(BF16)
    c0 = 0
    for ref, width in ((u_ref, POOL_W), (z_ref, SSD_INNER), (xbc_ref, SSD_XBC), (k_ref, ATT_W), (dt_ref, LANES)):
        ref[...] = _dot(xb, w_ref[:, c0:c0 + width]).astype(ref.dtype)
        c0 += width
    tr = _dot_nt(wt_ref[...], xb)
    qt_ref[...] = (tr[:ATT_W] * ATT_HDIM ** -0.5).astype(BF16)
    vt_ref[...] = tr[ATT_W:2 * ATT_W].astype(BF16)
    dtt_ref[...] = tr[2 * ATT_W:]


def _in_proj(h, w_all, w_t):
    t = h.shape[0]
    tm = TM_PROJ
    row = lambda w: pl.BlockSpec((tm, w), lambda i: (i, 0))
    col = lambda r: pl.BlockSpec((r, tm), lambda i: (0, i))
    full = lambda a: pl.BlockSpec(a.shape, lambda i: (0,) * a.ndim)
    return pl.pallas_call(
        _inproj_kernel,
        grid=(t // tm,),
        in_specs=[row(D_MODEL), full(w_all), full(w_t)],
        out_specs=[row(POOL_W), row(SSD_INNER), row(SSD_XBC), row(ATT_W), row(LANES),
                   col(ATT_W), col(ATT_W), col(DT_COLS)],
        out_shape=[jax.ShapeDtypeStruct((t, POOL_W), F32), jax.ShapeDtypeStruct((t, SSD_INNER), F32),
                   jax.ShapeDtypeStruct((t, SSD_XBC), F32), jax.ShapeDtypeStruct((t, ATT_W), BF16),
                   jax.ShapeDtypeStruct((t, LANES), F32), jax.ShapeDtypeStruct((ATT_W, t), BF16),
                   jax.ShapeDtypeStruct((ATT_W, t), BF16), jax.ShapeDtypeStruct((DT_COLS, t), F32)],
        compiler_params=_cparams(("parallel",)),
        name="in_proj",
    )(h, w_all, w_t)


def _seq_tables(seq_lens, tile):
    pos, length = [], []
    for s in seq_lens:
        assert s % tile == 0
        for p in range(0, s, tile):
            pos.append(p)
            length.append(s)
    return np.asarray(pos, np.int32), np.asarray(length, np.int32)


def _halo_specs(ts, width, n_tok):
    hb = ts // HALO
    last = n_tok // HALO - 1
    cur = pl.BlockSpec((ts, width), lambda i, *_: (i, 0))
    prev = pl.BlockSpec((HALO, width), lambda i, *_: (jnp.maximum(i * hb - 1, 0), 0))
    nxt = pl.BlockSpec((HALO, width), lambda i, *_: (jnp.minimum((i + 1) * hb, last), 0))
    return prev, cur, nxt


def _extended(prev_ref, cur_ref, next_ref, pos0, slen, ts):
    prev = jnp.where(pos0 > 0, prev_ref[...], 0.0)
    nxt = jnp.where(pos0 + ts < slen, next_ref[...], 0.0)
    return jnp.concatenate([prev, cur_ref[...], nxt], axis=0)


def _rows_from(ext, k, ts):
    n = ext.shape[0]
    shift = (n - (HALO + k)) % n
    rolled = ext if shift == 0 else pltpu.roll(ext, shift, 0)
    return rolled[:ts]


def _pool_kernel(pos_ref, len_ref, prev_ref, cur_ref, next_ref, w_ref, scale_ref, o_ref, *, ts):
    i = pl.program_id(0)
    pos0 = pos_ref[i]
    slen = len_ref[i]
    ext = _extended(prev_ref, cur_ref, next_ref, pos0, slen, ts)
    n = ext.shape[0]

    def pair_sum(a, k):
        return a + pltpu.roll(a, n - k, 0)

    a2 = pair_sum(ext, 1)
    a4 = pair_sum(a2, 2)
    a8 = pair_sum(a4, 4)
    a16 = pair_sum(a8, 8)
    sums = [_rows_from(a, -w // 2, ts) for a, w in zip((a2, a4, a8, a16), POOL_WINDOWS)]

    lane = lax.broadcasted_iota(jnp.int32, (ts, POOL_W), 1)
    grp = lane // POOL_GDIM
    wsum = jnp.where(grp == 0, sums[0], jnp.where(grp == 1, sums[1], jnp.where(grp == 2, sums[2], sums[3])))
    half = jnp.where(grp == 0, 1, jnp.where(grp == 1, 2, jnp.where(grp == 2, 4, 8)))
    tpos = pos0 + lax.broadcasted_iota(jnp.int32, (ts, POOL_W), 0)
    cnt = (jnp.minimum(tpos + half, slen) - jnp.maximum(tpos - half, 0)).astype(F32)
    pooled = wsum / cnt - cur_ref[...]
    o_ref[...] = _dot(pooled.astype(BF16), w_ref[...]) * scale_ref[...]


def _pool_mixer(u, w_bd, scale, seq_lens):
    t = u.shape[0]
    ts = TS_SEQ
    pos, length = _seq_tables(seq_lens, ts)
    prev, cur, nxt = _halo_specs(ts, POOL_W, t)
    grid_spec = pltpu.PrefetchScalarGridSpec(
        num_scalar_prefetch=2, grid=(t // ts,),
        in_specs=[prev, cur, nxt,
                  pl.BlockSpec((POOL_W, POOL_W), lambda i, *_: (0, 0)),
                  pl.BlockSpec((1, POOL_W), lambda i, *_: (0, 0))],
        out_specs=pl.BlockSpec((ts, POOL_W), lambda i, *_: (i, 0)))
    return pl.pallas_call(
        functools.partial(_pool_kernel, ts=ts), grid_spec=grid_spec,
        out_shape=jax.ShapeDtypeStruct((t, POOL_W), F32),
        compiler_params=_cparams(("parallel",)), name="pool_mixer",
    )(pos, length, u, u, u, w_bd, scale)


def _conv_kernel(pos_ref, len_ref, prev_ref, cur_ref, next_ref, dt_ref, w_ref, b_ref, dtb_ref, ef_ref, eb_ref,
                 xs_ref, xcf_ref, xcb_ref, c_ref, bt_ref, *, ts):
    i = pl.program_id(0)
    ext = _extended(prev_ref, cur_ref, next_ref, pos_ref[i], len_ref[i], ts)
    acc = jnp.zeros((ts, SSD_XBC), F32) + b_ref[...]
    for j in range(SSD_CONV):
        acc = acc + _rows_from(ext, j - SSD_CONV // 2, ts) * w_ref[j:j + 1, :]
    act = acc * jax.nn.sigmoid(acc)
    xs = act[:, :SSD_INNER]
    xs_ref[...] = xs
    dt_pk = _pack3(_softplus(dt_ref[...] + dtb_ref[...]))
    xcf_ref[...] = (xs * _dot(dt_pk, ef_ref[...])).astype(BF16)
    xcb_ref[...] = (xs * _dot(dt_pk, eb_ref[...])).astype(BF16)
    nb = SSD_GROUPS * SSD_STATE
    bt_ref[...] = act[:, SSD_INNER:SSD_INNER + nb].T.astype(BF16)
    c_ref[...] = act[:, SSD_INNER + nb:].astype(BF16)


def _conv_silu(xbc, dt, conv_w, conv_b, dt_bias_row, seq_lens):
    t = xbc.shape[0]
    ts = TS_SEQ
    pos, length = _seq_tables(seq_lens, ts)
    prev, cur, nxt = _halo_specs(ts, SSD_XBC, t)
    e_f = _expand_matrix(0, SSD_HDIM)
    e_b = _expand_matrix(SSD_HEADS, SSD_HDIM)
    nb = SSD_GROUPS * SSD_STATE
    const = lambda a: pl.BlockSpec(a.shape, lambda i, *_: (0, 0))
    row = lambda w: pl.BlockSpec((ts, w), lambda i, *_: (i, 0))
    grid_spec = pltpu.PrefetchScalarGridSpec(
        num_scalar_prefetch=2, grid=(t // ts,),
        in_specs=[prev, cur, nxt, row(LANES), const(conv_w), const(conv_b), const(dt_bias_row), const(e_f), const(e_b)],
        out_specs=[row(SSD_INNER), row(SSD_INNER), row(SSD_INNER), row(nb),
                   pl.BlockSpec((nb, ts), lambda i, *_: (0, i))])
    return pl.pallas_call(
        functools.partial(_conv_kernel, ts=ts), grid_spec=grid_spec,
        out_shape=[jax.ShapeDtypeStruct((t, SSD_INNER), F32), jax.ShapeDtypeStruct((t, SSD_INNER), BF16),
                   jax.ShapeDtypeStruct((t, SSD_INNER), BF16), jax.ShapeDtypeStruct((t, nb), BF16),
                   jax.ShapeDtypeStruct((nb, t), BF16)],
        compiler_params=_cparams(("parallel",)), name="conv_silu",
    )(pos, length, xbc, xbc, xbc, dt, conv_w, conv_b, dt_bias_row, e_f, e_b)


def _ssd_chunk(xc, c, bt, dt_full, dtt, aneg_full, aneg_col, e128, state_ref, reset, reverse, off):
    L = SSD_CHUNK
    ii = lax.broadcasted_iota(jnp.int32, (L, L), 0)
    jj = lax.broadcasted_iota(jnp.int32, (L, L), 1)
    tri = (jj >= ii) if reverse else (jj <= ii)
    tri_t = (ii >= jj) if reverse else (ii <= jj)
    tri_b = tri.astype(BF16)
    tri_tb = tri_t.astype(BF16)
    acs = sum(_dot(tri_b, part) for part in _split3(dt_full * aneg_full))
    acs_row = sum(_dot(part, tri_tb) for part in _split3(dtt * aneg_col))
    edge = 0 if reverse else L - 1
    acs_b = _dot(_pack3(acs), e128)
    lane = lax.broadcasted_iota(jnp.int32, (L, LANES), 1)
    low = lane < SSD_HDIM
    acs_e = jnp.concatenate(
        [jnp.where(low, acs_b[:, (2 * j) * LANES:(2 * j + 1) * LANES], acs_b[:, (2 * j + 1) * LANES:(2 * j + 2) * LANES])
         for j in range(SSD_HEADS // 2)], axis=1)
    tot_e = acs_e[edge:edge + 1, :]
    exp_acs = jnp.exp(acs_e)
    to_end = jnp.exp(tot_e - acs_e)
    chunk_decay = jnp.exp(tot_e)
    xw = (xc.astype(F32) * to_end).astype(BF16)
    zero_b = jnp.zeros((L, LANES), BF16)

    ys = []
    for g in range(SSD_GROUPS):
        gs = slice(g * SSD_GW, (g + 1) * SSD_GW)
        c_g = c[:, g * SSD_STATE:(g + 1) * SSD_STATE]
        bt_g = bt[g * SSD_STATE:(g + 1) * SSD_STATE, :]
        cb = _dot(c_g, bt_g)
        prev = jnp.where(reset, 0.0, state_ref[g])
        y_off = _dot(c_g, prev.astype(BF16)) * exp_acs[:, gs]
        state_ref[g] = prev * chunk_decay[:, gs] + _dot(bt_g, xw[:, gs])
        for pair in range(SSD_GW // LANES):
            t0 = g * SSD_GW + pair * LANES
            tile = xc[:, t0:t0 + LANES]
            halves = (jnp.where(low, tile, zero_b), jnp.where(low, zero_b, tile))
            y_pair = y_off[:, pair * LANES:(pair + 1) * LANES]
            for s in range(2):
                h = t0 // SSD_HDIM + s
                seg = acs_b[:, h * LANES:(h + 1) * LANES] - acs_row[off + h:off + h + 1, :]
                m = (cb * jnp.exp(jnp.where(tri, seg, NEG_BIG))).astype(BF16)
                y_pair = y_pair + _dot(m, halves[s])
            ys.append(y_pair)
    return jnp.concatenate(ys, axis=1)


def _ssd_kernel(mir_ref, start_ref, xc_f, c_f, bt_f, dt_f, dtt_f, xc_b, c_b, bt_b, dt_b, dtt_b,
                bias_row, bias_col, alog_row, alog_col, ef_ref, eb_ref, yf_ref, yb_ref, st_f, st_b):
    step = pl.program_id(0)
    reset = start_ref[step] == 1
    aneg_full = -jnp.exp(alog_row[...])
    aneg_col = -jnp.exp(alog_col[...])
    dtc_f = _softplus(dt_f[...] + bias_row[...])
    dtr_f = _softplus(dtt_f[...] + bias_col[...])
    dtc_b = _softplus(dt_b[...] + bias_row[...])
    dtr_b = _softplus(dtt_b[...] + bias_col[...])
    yf_ref[...] = _ssd_chunk(xc_f[...], c_f[...], bt_f[...], dtc_f, dtr_f, aneg_full, aneg_col, ef_ref[...],
                             st_f, reset, False, 0)
    yb_ref[...] = _ssd_chunk(xc_b[...], c_b[...], bt_b[...], dtc_b, dtr_b, aneg_full, aneg_col, eb_ref[...],
                             st_b, reset, True, SSD_HEADS)


def _ssd_tables(seq_lens):
    mirror, start = [], []
    c0 = 0
    for s in seq_lens:
        assert s % SSD_CHUNK == 0
        n = s // SSD_CHUNK
        for c in range(n):
            mirror.append(c0 + n - 1 - c)
            start.append(1 if c == 0 else 0)
        c0 += n
    return np.asarray(mirror, np.int32), np.asarray(start, np.int32)


def _ssd_scan(xcf, xcb, c, bt, dt, dtt, dt_bias_row, dt_bias_col, alog_row, alog_col, seq_lens):
    t = xcf.shape[0]
    L = SSD_CHUNK
    nb = SSD_GROUPS * SSD_STATE
    mirror, start = _ssd_tables(seq_lens)
    e_f = _expand_matrix(0, LANES)
    e_b = _expand_matrix(SSD_HEADS, LANES)
    fw = lambda w: pl.BlockSpec((L, w), lambda s, m, st: (s, 0))
    bw = lambda w: pl.BlockSpec((L, w), lambda s, m, st: (m[s], 0))
    fw_t = lambda r: pl.BlockSpec((r, L), lambda s, m, st: (0, s))
    bw_t = lambda r: pl.BlockSpec((r, L), lambda s, m, st: (0, m[s]))
    const = lambda a: pl.BlockSpec(a.shape, lambda s, m, st: (0, 0))
    consts = (dt_bias_row, dt_bias_col, alog_row, alog_col, e_f, e_b)
    grid_spec = pltpu.PrefetchScalarGridSpec(
        num_scalar_prefetch=2, grid=(t // L,),
        in_specs=[fw(SSD_INNER), fw(nb), fw_t(nb), fw(LANES), fw_t(DT_COLS),
                  bw(SSD_INNER), bw(nb), bw_t(nb), bw(LANES), bw_t(DT_COLS)] + [const(a) for a in consts],
        out_specs=[fw(SSD_INNER), bw(SSD_INNER)],
        scratch_shapes=[pltpu.VMEM((SSD_GROUPS, SSD_STATE, SSD_GW), F32),
                        pltpu.VMEM((SSD_GROUPS, SSD_STATE, SSD_GW), F32)])
    return pl.pallas_call(
        _ssd_kernel, grid_spec=grid_spec,
        out_shape=[jax.ShapeDtypeStruct((t, SSD_INNER), F32), jax.ShapeDtypeStruct((t, SSD_INNER), F32)],
        compiler_params=_cparams(("arbitrary",)), name="ssd_scan",
    )(mirror, start, xcf, c, bt, dt, dtt, xcb, c, bt, dt, dtt, *consts)


ATT_VARIANTS = 5
_ATT_RS_OFF = ((0, -1), (-2, -3), (-4, -4), (-4, -5), (-6, -7))


def _att_bias_table(rpb):
    kc = np.arange(GRID_W)[:, None]
    c = np.arange(GRID_W)[None, :]
    cs = np.clip(c - NA_COLS // 2, 0, GRID_W - NA_COLS)
    valid = (kc >= cs) & (kc < cs + NA_COLS)
    d = kc - c + NA_COLS - 1
    onehot = (d[None] == np.arange(2 * NA_COLS - 1)[:, None, None]) & valid[None]
    tt = jnp.einsum('hrd,dkc->hrkc', rpb.astype(F32), jnp.asarray(onehot, F32),
                    precision=jax.lax.Precision.HIGHEST)
    tt = jnp.where(valid, tt, NEG_BIG)
    neg = jnp.full((ATT_HEADS, GRID_W, GRID_W), NEG_BIG, F32)
    variants = []
    for var in range(ATT_VARIANTS):
        halves = []
        for dq in range(2):
            tiles = []
            for i in range(WIN_ROWS):
                delta = i - 2 * var - dq
                inside = _ATT_RS_OFF[var][dq] <= delta < _ATT_RS_OFF[var][dq] + NA_ROWS
                tiles.append(tt[:, delta + NA_ROWS - 1] if inside else neg)
            halves.append(jnp.concatenate(tiles, axis=1))
        variants.append(jnp.concatenate(halves, axis=2))
    return jnp.stack(variants)


def _att_kernel(qt_ref, k_ref, vt_ref, bias_ref, o_ref, *, rows, pb):
    j = pl.program_id(1)
    sub = lax.broadcasted_iota(jnp.int32, (LANES, LANES), 0)
    low = sub < ATT_HDIM
    zero = jnp.zeros((LANES, LANES), BF16)
    edge = NA_ROWS // 2
    for pp in range(pb):
        r = (j * pb + pp) * 2
        ws = jnp.clip(r - edge, 0, rows - WIN_ROWS)
        var = jnp.where(r < edge, r // 2, jnp.where(r >= rows - edge, (r - (rows - edge)) // 2 + 3, 2))
        k0 = pl.multiple_of(ws * GRID_W, LANES)
        kwin = k_ref[pl.ds(k0, WIN_KEYS), :]
        vtw = vt_ref[:, pl.ds(k0, WIN_KEYS)]
        qt = qt_ref[:, pp * LANES:(pp + 1) * LANES]
        outs = []
        for jp in range(ATT_HEADS // 2):
            qpair = qt[jp * LANES:(jp + 1) * LANES, :]
            qboth = jnp.concatenate([jnp.where(low, qpair, zero), jnp.where(low, zero, qpair)], axis=1)
            s2 = _dot(kwin[:, jp * LANES:(jp + 1) * LANES], qboth)
            for hh in range(2):
                h = 2 * jp + hh
                s = s2[:, hh * LANES:(hh + 1) * LANES] + bias_ref[var, h]
                m = jnp.max(s, axis=0, keepdims=True)
                p = jnp.exp(s - m)
                den = jnp.sum(p, axis=0, keepdims=True)
                outs.append(_dot(vtw[h * ATT_HDIM:(h + 1) * ATT_HDIM, :], p.astype(BF16)) / den)
        ot = jnp.concatenate(outs, axis=0)
        o_ref[pp * LANES:(pp + 1) * LANES, :] = ot.T.astype(o_ref.dtype)


def _attention(qt, k, vt, bias_tab, tok0, n_seq, seq_len):
    rows = seq_len // GRID_W
    pb = ATT_PAIRS_PER_STEP
    assert rows >= WIN_ROWS and rows % (2 * pb) == 0 and tok0 % seq_len == 0 and 2 * GRID_W == LANES
    steps = rows // (2 * pb)
    s0 = tok0 // seq_len
    qb0 = tok0 // (pb * LANES)
    return pl.pallas_call(
        functools.partial(_att_kernel, rows=rows, pb=pb),
        grid=(n_seq, steps),
        in_specs=[pl.BlockSpec((ATT_W, pb * LANES), lambda b, j: (0, qb0 + b * steps + j)),
                  pl.BlockSpec((seq_len, ATT_W), lambda b, j: (s0 + b, 0)),
                  pl.BlockSpec((ATT_W, seq_len), lambda b, j: (0, s0 + b)),
                  pl.BlockSpec(bias_tab.shape, lambda b, j: (0, 0, 0, 0))],
        out_specs=pl.BlockSpec((pb * LANES, ATT_W), lambda b, j: (b * steps + j, 0)),
        out_shape=jax.ShapeDtypeStruct((n_seq * seq_len, ATT_W), BF16),
        compiler_params=_cparams(("parallel", "parallel")), name="nbr_attention",
    )(qt, k, vt, bias_tab)


def _layer_norm(x, g, b):
    mu = jnp.mean(x, axis=-1, keepdims=True)
    xc = x - mu
    var = jnp.mean(xc * xc, axis=-1, keepdims=True)
    return xc * lax.rsqrt(var + LN_EPS) * g + b


def _outproj_kernel(h_ref, pool_ref, yf_ref, yb_ref, xs_ref, z_ref, att_ref, wo_ref, dskip_ref, ng_ref,
                    g_ref, b_ref, wr_ref, br_ref, h1_ref, h1b_ref, idx_ref, gate_ref, *, alpha):
    y = yf_ref[...] + yb_ref[...] + xs_ref[...] * dskip_ref[...]
    z = z_ref[...]
    y = y * (z * jax.nn.sigmoid(z))
    parts = []
    for g in range(SSD_GROUPS):
        yg = y[:, g * SSD_GW:(g + 1) * SSD_GW]
        parts.append(yg * lax.rsqrt(jnp.mean(yg * yg, axis=-1, keepdims=True) + RMS_EPS))
    ssd = (jnp.concatenate(parts, axis=1) * ng_ref[...]).astype(BF16)
    m = (_dot(pool_ref[...].astype(BF16), wo_ref[0:POOL_W, :])
         + _dot(ssd, wo_ref[POOL_W:POOL_W + SSD_INNER, :])
         + _dot(att_ref[...], wo_ref[POOL_W + SSD_INNER:, :]))
    h1 = _layer_norm(alpha * h_ref[...] + m, g_ref[...], b_ref[...])
    h1_ref[...] = h1
    h1b_ref[...] = h1.astype(BF16)

    logits = _dot_nt(wr_ref[...], h1, precision=jax.lax.Precision.HIGHEST) + br_ref[...]
    eidx = lax.broadcasted_iota(jnp.int32, logits.shape, 0)
    vals, idxs = [], []
    cur = logits
    for _ in range(TOP_K):
        mx = jnp.max(cur, axis=0, keepdims=True)
        ix = jnp.min(jnp.where(cur == mx, eidx, N_EXPERTS), axis=0, keepdims=True)
        vals.append(mx)
        idxs.append(ix)
        cur = jnp.where(eidx == ix, -jnp.inf, cur)
    es = [jnp.exp(vv - vals[0]) for vv in vals]
    den = es[0] + es[1] + es[2] + es[3]
    tm = logits.shape[1]
    idx_ref[...] = jnp.concatenate(idxs + [jnp.zeros((8 - TOP_K, tm), jnp.int32)], axis=0)
    gates_t = jnp.concatenate([e / den for e in es] + [jnp.zeros((LANES - TOP_K, tm), F32)], axis=0)
    gate_ref[...] = gates_t.T


def _out_proj_router(h, pool_out, y_f, y_b, xs, z, att, w_out, dskip, norm_g, ln_g, ln_b, w_router_t, b_router, alpha):
    t = h.shape[0]
    tm = TM_OUT
    row = lambda w: pl.BlockSpec((tm, w), lambda i: (i, 0))
    full = lambda a: pl.BlockSpec(a.shape, lambda i: (0,) * a.ndim)
    consts = (w_out, dskip, norm_g, ln_g, ln_b, w_router_t, b_router)
    return pl.pallas_call(
        functools.partial(_outproj_kernel, alpha=alpha),
        grid=(t // tm,),
        in_specs=[row(D_MODEL), row(POOL_W), row(SSD_INNER), row(SSD_INNER), row(SSD_INNER), row(SSD_INNER),
                  row(ATT_W)] + [full(a) for a in consts],
        out_specs=[row(D_MODEL), row(D_MODEL), pl.BlockSpec((8, tm), lambda i: (0, i)), row(LANES)],
        out_shape=[jax.ShapeDtypeStruct((t, D_MODEL), F32), jax.ShapeDtypeStruct((t, D_MODEL), BF16),
                   jax.ShapeDtypeStruct((8, t), jnp.int32), jax.ShapeDtypeStruct((t, LANES), F32)],
        compiler_params=_cparams(("parallel",)), name="out_proj_router",
    )(h, pool_out, y_f, y_b, xs, z, att, *consts)


def _expert_kernel(blk_ref, e_ref, lo_ref, hi_ref, first_ref, newe_ref, x_ref, w1_ref, b1g_ref, b1l_ref,
                   w2_ref, b2_ref, perm_ref, o_ref, w1g_s, w1l_s, w2_s):
    w = pl.program_id(0)
    lo = lo_ref[w]
    hi = hi_ref[w]

    @pl.when(newe_ref[w] == 1)
    def _():
        for j in range(D_FF // LANES):
            chunk = w1_ref[:, 2 * LANES * j:2 * LANES * (j + 1)].astype(BF16)
            r = _dot(chunk, perm_ref[...])
            w1g_s[:, LANES * j:LANES * (j + 1)] = r[:, :LANES].astype(BF16)
            w1l_s[:, LANES * j:LANES * (j + 1)] = r[:, LANES:].astype(BF16)
        w2_s[...] = w2_ref[...].astype(BF16)

    @pl.when(hi > lo)
    def _():
        x = x_ref[...]
        acc = jnp.zeros(o_ref.shape, F32) + b2_ref[...]
        for c in range(0, D_FF, FF_CHUNK):
            hg = _dot(x, w1g_s[:, c:c + FF_CHUNK]) + b1g_ref[:, c:c + FF_CHUNK]
            hl = _dot(x, w1l_s[:, c:c + FF_CHUNK]) + b1l_ref[:, c:c + FF_CHUNK]
            hg = jnp.minimum(hg, SWIGLU_LIMIT)
            hl = jnp.clip(hl, -SWIGLU_LIMIT, SWIGLU_LIMIT)
            act = hg * jax.nn.sigmoid(SWIGLU_ALPHA * hg) * (hl + 1.0)
            acc = acc + _dot(act.astype(BF16), w2_s[c:c + FF_CHUNK, :])
        rows = blk_ref[w] * MOE_BM + lax.broadcasted_iota(jnp.int32, (MOE_BM, 1), 0)
        mine = (rows >= lo) & (rows < hi)
        y = acc.astype(o_ref.dtype)

        @pl.when(first_ref[w] == 1)
        def _():
            o_ref[...] = jnp.where(mine, y, jnp.zeros_like(y))

        @pl.when(first_ref[w] == 0)
        def _():
            o_ref[...] = jnp.where(mine, y, o_ref[...])


def _experts(x_sorted, meta, layer, w1, b1g, b1l, w2, b2):
    n_rows = x_sorted.shape[0]
    bm = MOE_BM
    n_items = meta[0].shape[0]
    o = np.arange(2 * LANES)[None, :]
    c = np.arange(2 * LANES)[:, None]
    perm = jnp.asarray(np.where(o < LANES, c == 2 * o, c == 2 * (o - LANES) + 1), BF16)
    wspec = lambda r, c: pl.BlockSpec((None, None, r, c), lambda w, blk, e, *_: (layer, e[w], 0, 0))
    bspec = lambda c: pl.BlockSpec((None, 1, c), lambda w, blk, e, *_: (e[w], 0, 0))
    grid_spec = pltpu.PrefetchScalarGridSpec(
        num_scalar_prefetch=6, grid=(n_items,),
        in_specs=[pl.BlockSpec((bm, D_MODEL), lambda w, blk, *_: (blk[w], 0)),
                  wspec(D_MODEL, 2 * D_FF), bspec(D_FF), bspec(D_FF), wspec(D_FF, D_MODEL), bspec(D_MODEL),
                  pl.BlockSpec(perm.shape, lambda w, *_: (0, 0))],
        out_specs=pl.BlockSpec((bm, D_MODEL), lambda w, blk, *_: (blk[w], 0)),
        scratch_shapes=[pltpu.VMEM((D_MODEL, D_FF), BF16), pltpu.VMEM((D_MODEL, D_FF), BF16),
                        pltpu.VMEM((D_FF, D_MODEL), BF16)])
    return pl.pallas_call(
        _expert_kernel, grid_spec=grid_spec,
        out_shape=jax.ShapeDtypeStruct((n_rows, D_MODEL), BF16),
        compiler_params=_cparams(("arbitrary",)), name="moe_experts",
    )(*meta, x_sorted, w1, b1g, b1l, w2, b2, perm)


def _combine_kernel(h_ref, y_ref, gate_ref, g_ref, b_ref, o_ref, *, alpha):
    gates = gate_ref[...]
    f = y_ref[0].astype(F32) * gates[:, 0:1]
    for k in range(1, TOP_K):
        f = f + y_ref[k].astype(F32) * gates[:, k:k + 1]
    o_ref[...] = _layer_norm(alpha * h_ref[...] + f, g_ref[...], b_ref[...])


def _combine_ln(h1, y_tok, gate_col, ln_g, ln_b, alpha, tok0, n_tok):
    tm = TM_OUT
    assert tok0 % tm == 0 and n_tok % tm == 0
    b0 = tok0 // tm
    return pl.pallas_call(
        functools.partial(_combine_kernel, alpha=alpha),
        grid=(n_tok // tm,),
        in_specs=[pl.BlockSpec((tm, D_MODEL), lambda i: (b0 + i, 0)),
                  pl.BlockSpec((TOP_K, tm, D_MODEL), lambda i: (0, b0 + i, 0)),
                  pl.BlockSpec((tm, LANES), lambda i: (b0 + i, 0)),
                  pl.BlockSpec((1, D_MODEL), lambda i: (0, 0)),
                  pl.BlockSpec((1, D_MODEL), lambda i: (0, 0))],
        out_specs=pl.BlockSpec((tm, D_MODEL), lambda i: (i, 0)),
        out_shape=jax.ShapeDtypeStruct((n_tok, D_MODEL), F32),
        compiler_params=_cparams(("parallel",)), name="combine_ln",
    )(h1, y_tok, gate_col, ln_g, ln_b)


def _routing(idx_t):
    t = idx_t.shape[1]
    n_asg = TOP_K * t
    bm = MOE_BM
    n_blocks = n_asg // bm
    n_items = n_blocks + N_EXPERTS - 1
    i32 = jnp.int32
    flat_e = idx_t[:TOP_K].reshape(-1)
    a = jnp.arange(n_asg, dtype=i32)
    skeys, order = lax.sort((flat_e * n_asg + a, a), num_keys=1)
    _, pos = lax.sort((order, a), num_keys=1)
    src_tok = order % t
    edges = jnp.arange(N_EXPERTS + 1, dtype=i32) * n_asg
    bounds = jnp.sum((skeys[None, :] < edges[:, None]).astype(i32), axis=1)
    starts, ends = bounds[:-1], bounds[1:]
    first_tile = starts // bm
    ntiles = jnp.where(ends > starts, (ends + bm - 1) // bm - first_tile, 0)
    cum = jnp.cumsum(ntiles)
    base = cum - ntiles
    total = cum[-1]
    w = jnp.arange(n_items, dtype=i32)
    valid = w < total
    e_w = jnp.minimum(jnp.sum((cum[None, :] <= w[:, None]).astype(i32), axis=1), N_EXPERTS - 1)
    e_last = e_w[jnp.maximum(total - 1, 0)]
    e_w = jnp.where(valid, e_w, e_last)
    blk = jnp.where(valid, first_tile[e_w] + (w - base[e_w]), n_blocks - 1).astype(i32)
    lo = jnp.where(valid, jnp.maximum(starts[e_w], blk * bm), 0).astype(i32)
    hi = jnp.where(valid, jnp.minimum(ends[e_w], (blk + 1) * bm), 0).astype(i32)
    one = jnp.ones((1,), i32)
    first = jnp.concatenate([one, (blk[1:] != blk[:-1]).astype(i32)])
    new_e = jnp.concatenate([one, (e_w[1:] != e_w[:-1]).astype(i32)])
    return src_tok, pos, (blk, e_w, lo, hi, first, new_e)


def _prep_layer(i, w_in, conv_w, conv_b, a_log, dt_bias, w_pool, pool_scale, rpb, w_out, w_router, b_router,
                w1, b1, w2, b2, d_skip, ssd_norm_g, ln1_g, ln1_b, ln2_g, ln2_b):
    c_x = POOL_W + SSD_INNER + SSD_XBC
    c_dt = c_x + DT_COLS
    w = w_in[i]
    w_dt = w[:, c_x:c_dt]
    w_q, w_k, w_v = (w[:, c_dt + j * ATT_W:c_dt + (j + 1) * ATT_W] for j in range(3))
    pad = LANES - DT_REP * DT_COLS
    w_all = jnp.concatenate([w[:, :c_x], w_k] + [w_dt] * DT_REP + [jnp.zeros((D_MODEL, pad), F32)],
                            axis=1).astype(BF16)
    w_t = jnp.concatenate([w_q, w_v, w_dt], axis=1).T.astype(BF16)
    packed_row = lambda v: jnp.concatenate([v.reshape(1, DT_COLS)] * DT_REP + [jnp.zeros((1, pad), F32)], axis=1)
    w_bd = jnp.zeros((POOL_W, POOL_W), F32)
    for g in range(POOL_GROUPS):
        sl = slice(g * POOL_GDIM, (g + 1) * POOL_GDIM)
        w_bd = w_bd.at[sl, sl].set(w_pool[i, g])
    conv_w_pad = jnp.concatenate([conv_w[i], jnp.zeros((HALO - SSD_CONV, SSD_XBC), F32)], axis=0)
    return dict(
        layer=i, w1=w1, w2=w2,
        w_all=w_all, w_t=w_t, w_bd=w_bd.astype(BF16), pool_scale=pool_scale[i].reshape(1, POOL_W),
        conv_w=conv_w_pad, conv_b=conv_b[i].reshape(1, SSD_XBC),
        dt_bias_row=packed_row(dt_bias[i]), dt_bias_col=dt_bias[i].reshape(DT_COLS, 1),
        alog_row=packed_row(a_log[i]), alog_col=a_log[i].reshape(DT_COLS, 1),
        bias_tab=_att_bias_table(rpb[i]),
        w_out=w_out[i].astype(BF16), dskip=jnp.repeat(d_skip[i], SSD_HDIM).reshape(1, SSD_INNER),
        norm_g=ssd_norm_g[i].reshape(1, SSD_INNER), ln1_g=ln1_g[i].reshape(1, D_MODEL), ln1_b=ln1_b[i].reshape(1, D_MODEL),
        w_router_t=w_router[i].T, b_router=b_router[i].reshape(N_EXPERTS, 1),
        b1g=b1[i, :, 0::2].reshape(N_EXPERTS, 1, D_FF), b1l=b1[i, :, 1::2].reshape(N_EXPERTS, 1, D_FF),
        b2=b2[i].reshape(N_EXPERTS, 1, D_MODEL),
        ln2_g=ln2_g[i].reshape(1, D_MODEL), ln2_b=ln2_b[i].reshape(1, D_MODEL))


def _encoder_layer(h, p, seq_groups, alpha, split_out):
    t = h.shape[0]
    seq_lens = tuple(s for tok0, n, s in seq_groups for _ in range(n))
    u, z, xbc, k, dt, qt, vt, dtt = _in_proj(h, p['w_all'], p['w_t'])
    pool_out = _pool_mixer(u, p['w_bd'], p['pool_scale'], seq_lens)
    xs, xcf, xcb, c, bt = _conv_silu(xbc, dt, p['conv_w'], p['conv_b'], p['dt_bias_row'], seq_lens)
    y_f, y_b = _ssd_scan(xcf, xcb, c, bt, dt, dtt, p['dt_bias_row'], p['dt_bias_col'], p['alog_row'], p['alog_col'],
                         seq_lens)
    att = jnp.concatenate([_attention(qt, k, vt, p['bias_tab'], tok0, n, s) for tok0, n, s in seq_groups], axis=0)
    h1, h1b, idx_t, gate_col = _out_proj_router(h, pool_out, y_f, y_b, xs, z, att, p['w_out'], p['dskip'], p['norm_g'],
                                                p['ln1_g'], p['ln1_b'], p['w_router_t'], p['b_router'], alpha)
    src_tok, pos, meta = _routing(idx_t)
    x_sorted = jnp.take(h1b, src_tok, axis=0, mode='clip')
    y_sorted = _experts(x_sorted, meta, p['layer'], p['w1'], p['b1g'], p['b1l'], p['w2'], p['b2'])
    y_tok = jnp.take(y_sorted, pos, axis=0, mode='clip').reshape(TOP_K, t, D_MODEL)
    ranges = [(tok0, n * s) for tok0, n, s in seq_groups] if split_out else [(0, t)]
    return [_combine_ln(h1, y_tok, gate_col, p['ln2_g'], p['ln2_b'], alpha, tok0, n_tok) for tok0, n_tok in ranges]


def kernel(x_prompt, x_sample, w_in, conv_w, conv_b, a_log, dt_bias, d_skip, ssd_norm_g, w_pool, pool_scale, rpb, w_out,
           ln1_g, ln1_b, w_router, b_router, w1, b1, w2, b2, ln2_g, ln2_b):
    depth = w_in.shape[0]
    alpha = (2 * depth) ** 0.25
    bp, sp, _ = x_prompt.shape
    bs, ss, _ = x_sample.shape
    seq_groups = ((0, bp, sp), (bp * sp, bs, ss))
    h = jnp.concatenate([x_prompt.reshape(bp * sp, D_MODEL), x_sample.reshape(bs * ss, D_MODEL)], axis=0)
    for i in range(depth):
        p = _prep_layer(i, w_in, conv_w, conv_b, a_log, dt_bias, w_pool, pool_scale, rpb, w_out, w_router, b_router,
                        w1, b1, w2, b2, d_skip, ssd_norm_g, ln1_g, ln1_b, ln2_g, ln2_b)
        outs = _encoder_layer(h, p, seq_groups, alpha, split_out=(i == depth - 1))
        h = outs[0]
    return (outs[0].reshape(bp, sp, D_MODEL), outs[1].reshape(bs, ss, D_MODEL))
```

```python
import functools

import numpy as np
import jax
import jax.numpy as jnp
from jax import lax
from jax.experimental import pallas as pl
from jax.experimental.pallas import tpu as pltpu

F32 = jnp.float32
BF16 = jnp.bfloat16

D_MODEL = 1024
POOL_GROUPS = 4
POOL_GDIM = 64
POOL_W = POOL_GROUPS * POOL_GDIM
POOL_WINDOWS = (2, 4, 8, 16)
SSD_HEADS = 8
SSD_HDIM = 64
SSD_INNER = SSD_HEADS * SSD_HDIM
SSD_GROUPS = 2
SSD_STATE = 128
SSD_CONV = 5
SSD_CHUNK = 128
SSD_BC = 2 * SSD_GROUPS * SSD_STATE
SSD_XBC = SSD_INNER + SSD_BC
SSD_GW = SSD_INNER // SSD_GROUPS
ATT_HEADS = 4
ATT_HDIM = 64
ATT_W = ATT_HEADS * ATT_HDIM
GRID_W = 64
NA_ROWS = 8
NA_COLS = 16
N_EXPERTS = 32
TOP_K = 4
D_FF = D_MODEL
SWIGLU_LIMIT = 7.0
SWIGLU_ALPHA = 1.702
LN_EPS = 1e-5
RMS_EPS = 1e-5

LANES = 128
DT_COLS = 2 * SSD_HEADS
DT_REP = 3
HALO = 8
NEG_BIG = -1e30
WIN_ROWS = NA_ROWS + 2
WIN_KEYS = WIN_ROWS * GRID_W

TM_PROJ = 512
TM_OUT = 512
SSD_CHUNKS_PER_STEP = 2
TS_SEQ = 256
ATT_PAIRS_PER_STEP = 4
MOE_BM = 512
FF_CHUNK = 512
VMEM_LIMIT = 56 * 1024 * 1024


def _cparams(sem):
    return pltpu.CompilerParams(dimension_semantics=sem, vmem_limit_bytes=VMEM_LIMIT)


def _dot(a, b):
    return jnp.dot(a, b, preferred_element_type=F32)


def _dot_nt(a, b, precision=None):
    return lax.dot_general(a, b, (((1,), (1,)), ((), ())), preferred_element_type=F32, precision=precision)


def _softplus(v):
    return jnp.maximum(v, 0.0) + jnp.log1p(jnp.exp(-jnp.abs(v)))


def _split3(v):
    hi = v.astype(BF16)
    r1 = v - hi.astype(F32)
    mid = r1.astype(BF16)
    lo = (r1 - mid.astype(F32)).astype(BF16)
    return hi, mid, lo


def _pack3(v):
    hi, mid, lo = _split3(v)
    lane = lax.broadcasted_iota(jnp.int32, v.shape, 1)
    zero = jnp.zeros(v.shape, BF16)
    return jnp.where(lane < DT_COLS, hi, jnp.where(lane < 2 * DT_COLS, mid, jnp.where(lane < 3 * DT_COLS, lo, zero)))


def _expand_matrix(off, width):
    r = np.arange(LANES)[:, None]
    c = np.arange(SSD_HEADS * width)[None, :]
    m = (r < DT_REP * DT_COLS) & ((r % DT_COLS) == off + c // width)
    return jnp.asarray(m, BF16)


def _inproj_kernel(x_ref, w_ref, wt_ref, u_ref, z_ref, xbc_ref, k_ref, dt_ref, qt_ref, vt_ref, dtt_ref):
    xb = x_ref[...].astype(BF16)
    c0 = 0
    for ref, width in ((u_ref, POOL_W), (z_ref, SSD_INNER), (xbc_ref, SSD_XBC), (k_ref, ATT_W), (dt_ref, LANES)):
        ref[...] = _dot(xb, w_ref[:, c0:c0 + width]).astype(ref.dtype)
        c0 += width
    tr = _dot_nt(wt_ref[...], xb)
    qt_ref[...] = (tr[:ATT_W] * ATT_HDIM ** -0.5).astype(BF16)
    vt_ref[...] = tr[ATT_W:2 * ATT_W].astype(BF16)
    dtt_ref[...] = tr[2 * ATT_W:]


def _in_proj(h, w_all, w_t):
    t = h.shape[0]
    tm = TM_PROJ
    row = lambda w: pl.BlockSpec((tm, w), lambda i: (i, 0))
    col = lambda r: pl.BlockSpec((r, tm), lambda i: (0, i))
    full = lambda a: pl.BlockSpec(a.shape, lambda i: (0,) * a.ndim)
    return pl.pallas_call(
        _inproj_kernel,
        grid=(t // tm,),
        in_specs=[row(D_MODEL), full(w_all), full(w_t)],
        out_specs=[row(POOL_W), row(SSD_INNER), row(SSD_XBC), row(ATT_W), row(LANES),
                   col(ATT_W), col(ATT_W), col(DT_COLS)],
        out_shape=[jax.ShapeDtypeStruct((t, POOL_W), F32), jax.ShapeDtypeStruct((t, SSD_INNER), F32),
                   jax.ShapeDtypeStruct((t, SSD_XBC), F32), jax.ShapeDtypeStruct((t, ATT_W), BF16),
                   jax.ShapeDtypeStruct((t, LANES), F32), jax.ShapeDtypeStruct((ATT_W, t), BF16),
                   jax.ShapeDtypeStruct((ATT_W, t), BF16), jax.ShapeDtypeStruct((DT_COLS, t), F32)],
        compiler_params=_cparams(("parallel",)),
        name="in_proj",
    )(h, w_all, w_t)


def _seq_tables(seq_lens, tile):
    pos, length = [], []
    for s in seq_lens:
        assert s % tile == 0
        for p in range(0, s, tile):
            pos.append(p)
            length.append(s)
    return np.asarray(pos, np.int32), np.asarray(length, np.int32)


def _halo_specs(ts, width, n_tok):
    hb = ts // HALO
    last = n_tok // HALO - 1
    cur = pl.BlockSpec((ts, width), lambda i, *_: (i, 0))
    prev = pl.BlockSpec((HALO, width), lambda i, *_: (jnp.maximum(i * hb - 1, 0), 0))
    nxt = pl.BlockSpec((HALO, width), lambda i, *_: (jnp.minimum((i + 1) * hb, last), 0))
    return prev, cur, nxt


def _extended(prev_ref, cur_ref, next_ref, pos0, slen, ts):
    prev = jnp.where(pos0 > 0, prev_ref[...], 0.0)
    nxt = jnp.where(pos0 + ts < slen, next_ref[...], 0.0)
    return jnp.concatenate([prev, cur_ref[...], nxt], axis=0)


def _rows_from(ext, k, ts):
    n = ext.shape[0]
    shift = (n - (HALO + k)) % n
    rolled = ext if shift == 0 else pltpu.roll(ext, shift, 0)
    return rolled[:ts]


def _pool_kernel(pos_ref, len_ref, prev_ref, cur_ref, next_ref, w_ref, scale_ref, o_ref, *, ts):
    i = pl.program_id(0)
    pos0 = pos_ref[i]
    slen = len_ref[i]
    ext = _extended(prev_ref, cur_ref, next_ref, pos0, slen, ts)
    n = ext.shape[0]

    def pair_sum(a, k):
        return a + pltpu.roll(a, n - k, 0)

    a2 = pair_sum(ext, 1)
    a4 = pair_sum(a2, 2)
    a8 = pair_sum(a4, 4)
    a16 = pair_sum(a8, 8)
    sums = [_rows_from(a, -w // 2, ts) for a, w in zip((a2, a4, a8, a16), POOL_WINDOWS)]

    lane = lax.broadcasted_iota(jnp.int32, (ts, POOL_W), 1)
    grp = lane // POOL_GDIM
    wsum = jnp.where(grp == 0, sums[0], jnp.where(grp == 1, sums[1], jnp.where(grp == 2, sums[2], sums[3])))
    half = jnp.where(grp == 0, 1, jnp.where(grp == 1, 2, jnp.where(grp == 2, 4, 8)))
    tpos = pos0 + lax.broadcasted_iota(jnp.int32, (ts, POOL_W), 0)
    cnt = (jnp.minimum(tpos + half, slen) - jnp.maximum(tpos - half, 0)).astype(F32)
    pooled = wsum / cnt - cur_ref[...]
    o_ref[...] = _dot(pooled.astype(BF16), w_ref[...]) * scale_ref[...]


def _pool_mixer(u, w_bd, scale, seq_lens):
    t = u.shape[0]
    ts = TS_SEQ
    pos, length = _seq_tables(seq_lens, ts)
    prev, cur, nxt = _halo_specs(ts, POOL_W, t)
    grid_spec = pltpu.PrefetchScalarGridSpec(
        num_scalar_prefetch=2, grid=(t // ts,),
        in_specs=[prev, cur, nxt,
                  pl.BlockSpec((POOL_W, POOL_W), lambda i, *_: (0, 0)),
                  pl.BlockSpec((1, POOL_W), lambda i, *_: (0, 0))],
        out_specs=pl.BlockSpec((ts, POOL_W), lambda i, *_: (i, 0)))
    return pl.pallas_call(
        functools.partial(_pool_kernel, ts=ts), grid_spec=grid_spec,
        out_shape=jax.ShapeDtypeStruct((t, POOL_W), F32),
        compiler_params=_cparams(("parallel",)), name="pool_mixer",
    )(pos, length, u, u, u, w_bd, scale)


def _conv_kernel(pos_ref, len_ref, prev_ref, cur_ref, next_ref, dt_ref, w_ref, b_ref, dtb_ref, ef_ref, eb_ref,
                 xs_ref, xcf_ref, xcb_ref, c_ref, bt_ref, *, ts):
    i = pl.program_id(0)
    ext = _extended(prev_ref, cur_ref, next_ref, pos_ref[i], len_ref[i], ts)
    acc = jnp.zeros((ts, SSD_XBC), F32) + b_ref[...]
    for j in range(SSD_CONV):
        acc = acc + _rows_from(ext, j - SSD_CONV // 2, ts) * w_ref[j:j + 1, :]
    act = acc * jax.nn.sigmoid(acc)
    xs = act[:, :SSD_INNER]
    xs_ref[...] = xs
    dt_pk = _pack3(_softplus(dt_ref[...] + dtb_ref[...]))
    xcf_ref[...] = (xs * _dot(dt_pk, ef_ref[...])).astype(BF16)
    xcb_ref[...] = (xs * _dot(dt_pk, eb_ref[...])).astype(BF16)
    nb = SSD_GROUPS * SSD_STATE
    bt_ref[...] = act[:, SSD_INNER:SSD_INNER + nb].T.astype(BF16)
    c_ref[...] = act[:, SSD_INNER + nb:].astype(BF16)


def _conv_silu(xbc, dt, conv_w, conv_b, dt_bias_row, seq_lens):
    t = xbc.shape[0]
    ts = TS_SEQ
    pos, length = _seq_tables(seq_lens, ts)
    prev, cur, nxt = _halo_specs(ts, SSD_XBC, t)
    e_f = _expand_matrix(0, SSD_HDIM)
    e_b = _expand_matrix(SSD_HEADS, SSD_HDIM)
    nb = SSD_GROUPS * SSD_STATE
    const = lambda a: pl.BlockSpec(a.shape, lambda i, *_: (0, 0))
    row = lambda w: pl.BlockSpec((ts, w), lambda i, *_: (i, 0))
    grid_spec = pltpu.PrefetchScalarGridSpec(
        num_scalar_prefetch=2, grid=(t // ts,),
        in_specs=[prev, cur, nxt, row(LANES), const(conv_w), const(conv_b), const(dt_bias_row), const(e_f), const(e_b)],
        out_specs=[row(SSD_INNER), row(SSD_INNER), row(SSD_INNER), row(nb),
                   pl.BlockSpec((nb, ts), lambda i, *_: (0, i))])
    return pl.pallas_call(
        functools.partial(_conv_kernel, ts=ts), grid_spec=grid_spec,
        out_shape=[jax.ShapeDtypeStruct((t, SSD_INNER), F32), jax.ShapeDtypeStruct((t, SSD_INNER), BF16),
                   jax.ShapeDtypeStruct((t, SSD_INNER), BF16), jax.ShapeDtypeStruct((t, nb), BF16),
                   jax.ShapeDtypeStruct((nb, t), BF16)],
        compiler_params=_cparams(("parallel",)), name="conv_silu",
    )(pos, length, xbc, xbc, xbc, dt, conv_w, conv_b, dt_bias_row, e_f, e_b)


def _ssd_chunk(xc, c, bt, dt_full, dtt, aneg_full, aneg_col, e128, state_ref, reset, reverse, off):
    L = SSD_CHUNK
    ii = lax.broadcasted_iota(jnp.int32, (L, L), 0)
    jj = lax.broadcasted_iota(jnp.int32, (L, L), 1)
    tri = (jj >= ii) if reverse else (jj <= ii)
    tri_t = (ii >= jj) if reverse else (ii <= jj)
    tri_b = tri.astype(BF16)
    tri_tb = tri_t.astype(BF16)
    acs = sum(_dot(tri_b, part) for part in _split3(dt_full * aneg_full))
    acs_row = sum(_dot(part, tri_tb) for part in _split3(dtt * aneg_col))
    edge = 0 if reverse else L - 1
    acs_b = _dot(_pack3(acs), e128)
    lane = lax.broadcasted_iota(jnp.int32, (L, LANES), 1)
    low = lane < SSD_HDIM
    acs_e = jnp.concatenate(
        [jnp.where(low, acs_b[:, (2 * j) * LANES:(2 * j + 1) * LANES], acs_b[:, (2 * j + 1) * LANES:(2 * j + 2) * LANES])
         for j in range(SSD_HEADS // 2)], axis=1)
    tot_e = acs_e[edge:edge + 1, :]
    exp_acs = jnp.exp(acs_e)
    to_end = jnp.exp(tot_e - acs_e)
    chunk_decay = jnp.exp(tot_e)
    xw = (xc.astype(F32) * to_end).astype(BF16)
    zero_b = jnp.zeros((L, LANES), BF16)

    ys = []
    for g in range(SSD_GROUPS):
        gs = slice(g * SSD_GW, (g + 1) * SSD_GW)
        c_g = c[:, g * SSD_STATE:(g + 1) * SSD_STATE]
        bt_g = bt[g * SSD_STATE:(g + 1) * SSD_STATE, :]
        cb = _dot(c_g, bt_g)
        prev = state_ref[g] if reset is None else jnp.where(reset, 0.0, state_ref[g])
        y_off = _dot(c_g, prev.astype(BF16)) * exp_acs[:, gs]
        state_ref[g] = prev * chunk_decay[:, gs] + _dot(bt_g, xw[:, gs])
        for pair in range(SSD_GW // LANES):
            t0 = g * SSD_GW + pair * LANES
            tile = xc[:, t0:t0 + LANES]
            halves = (jnp.where(low, tile, zero_b), jnp.where(low, zero_b, tile))
            y_pair = y_off[:, pair * LANES:(pair + 1) * LANES]
            for s in range(2):
                h = t0 // SSD_HDIM + s
                seg = acs_b[:, h * LANES:(h + 1) * LANES] - acs_row[off + h:off + h + 1, :]
                m = (cb * jnp.exp(jnp.where(tri, seg, NEG_BIG))).astype(BF16)
                y_pair = y_pair + _dot(m, halves[s])
            ys.append(y_pair)
    return jnp.concatenate(ys, axis=1)


def _ssd_kernel(mir_ref, start_ref, xc_f, c_f, bt_f, dt_f, dtt_f, xc_b, c_b, bt_b, dt_b, dtt_b,
                bias_row, bias_col, alog_row, alog_col, ef_ref, eb_ref, yf_ref, yb_ref, st_f, st_b):
    step = pl.program_id(0)
    L = SSD_CHUNK
    first = start_ref[step] == 1
    aneg_full = -jnp.exp(alog_row[...])
    aneg_col = -jnp.exp(alog_col[...])
    dirs = ((xc_f, c_f, bt_f, dt_f, dtt_f, ef_ref, yf_ref, st_f, False, 0),
            (xc_b, c_b, bt_b, dt_b, dtt_b, eb_ref, yb_ref, st_b, True, SSD_HEADS))
    for n in range(SSD_CHUNKS_PER_STEP):
        for xc, c, bt, dt, dtt, e_ref, y_ref, st, reverse, off in dirs:
            sub = SSD_CHUNKS_PER_STEP - 1 - n if reverse else n
            rows = slice(sub * L, (sub + 1) * L)
            dtc = _softplus(dt[rows, :] + bias_row[...])
            dtr = _softplus(dtt[:, rows] + bias_col[...])
            reset = first if n == 0 else None
            y_ref[rows, :] = _ssd_chunk(xc[rows, :], c[rows, :], bt[:, rows], dtc, dtr, aneg_full, aneg_col, e_ref[...],
                                        st, reset, reverse, off)


def _ssd_tables(seq_lens, block):
    mirror, start = [], []
    c0 = 0
    for s in seq_lens:
        assert s % block == 0
        n = s // block
        for c in range(n):
            mirror.append(c0 + n - 1 - c)
            start.append(1 if c == 0 else 0)
        c0 += n
    return np.asarray(mirror, np.int32), np.asarray(start, np.int32)


def _ssd_scan(xcf, xcb, c, bt, dt, dtt, dt_bias_row, dt_bias_col, alog_row, alog_col, seq_lens):
    t = xcf.shape[0]
    L = SSD_CHUNK * SSD_CHUNKS_PER_STEP
    nb = SSD_GROUPS * SSD_STATE
    mirror, start = _ssd_tables(seq_lens, L)
    e_f = _expand_matrix(0, LANES)
    e_b = _expand_matrix(SSD_HEADS, LANES)
    fw = lambda w: pl.BlockSpec((L, w), lambda s, m, st: (s, 0))
    bw = lambda w: pl.BlockSpec((L, w), lambda s, m, st: (m[s], 0))
    fw_t = lambda r: pl.BlockSpec((r, L), lambda s, m, st: (0, s))
    bw_t = lambda r: pl.BlockSpec((r, L), lambda s, m, st: (0, m[s]))
    const = lambda a: pl.BlockSpec(a.shape, lambda s, m, st: (0, 0))
    consts = (dt_bias_row, dt_bias_col, alog_row, alog_col, e_f, e_b)
    grid_spec = pltpu.PrefetchScalarGridSpec(
        num_scalar_prefetch=2, grid=(t // L,),
        in_specs=[fw(SSD_INNER), fw(nb), fw_t(nb), fw(LANES), fw_t(DT_COLS),
                  bw(SSD_INNER), bw(nb), bw_t(nb), bw(LANES), bw_t(DT_COLS)] + [const(a) for a in consts],
        out_specs=[fw(SSD_INNER), bw(SSD_INNER)],
        scratch_shapes=[pltpu.VMEM((SSD_GROUPS, SSD_STATE, SSD_GW), F32),
                        pltpu.VMEM((SSD_GROUPS, SSD_STATE, SSD_GW), F32)])
    return pl.pallas_call(
        _ssd_kernel, grid_spec=grid_spec,
        out_shape=[jax.ShapeDtypeStruct((t, SSD_INNER), F32), jax.ShapeDtypeStruct((t, SSD_INNER), F32)],
        compiler_params=_cparams(("arbitrary",)), name="ssd_scan",
    )(mirror, start, xcf, c, bt, dt, dtt, xcb, c, bt, dt, dtt, *consts)


ATT_VARIANTS = 5
_ATT_RS_OFF = ((0, -1), (-2, -3), (-4, -4), (-4, -5), (-6, -7))


def _att_bias_table(rpb):
    kc = np.arange(GRID_W)[:, None]
    c = np.arange(GRID_W)[None, :]
    cs = np.clip(c - NA_COLS // 2, 0, GRID_W - NA_COLS)
    valid = (kc >= cs) & (kc < cs + NA_COLS)
    d = kc - c + NA_COLS - 1
    onehot = (d[None] == np.arange(2 * NA_COLS - 1)[:, None, None]) & valid[None]
    tt = jnp.einsum('hrd,dkc->hrkc', rpb.astype(F32), jnp.asarray(onehot, F32),
                    precision=jax.lax.Precision.HIGHEST)
    tt = jnp.where(valid, tt, NEG_BIG)
    neg = jnp.full((ATT_HEADS, GRID_W, GRID_W), NEG_BIG, F32)
    variants = []
    for var in range(ATT_VARIANTS):
        halves = []
        for dq in range(2):
            tiles = []
            for i in range(WIN_ROWS):
                delta = i - 2 * var - dq
                inside = _ATT_RS_OFF[var][dq] <= delta < _ATT_RS_OFF[var][dq] + NA_ROWS
                tiles.append(tt[:, delta + NA_ROWS - 1] if inside else neg)
            halves.append(jnp.concatenate(tiles, axis=1))
        variants.append(jnp.concatenate(halves, axis=2))
    return jnp.stack(variants)


def _att_kernel(qt_ref, k_ref, vt_ref, bias_ref, o_ref, *, rows, pb):
    j = pl.program_id(1)
    sub = lax.broadcasted_iota(jnp.int32, (LANES, LANES), 0)
    low = sub < ATT_HDIM
    zero = jnp.zeros((LANES, LANES), BF16)
    edge = NA_ROWS // 2
    for pp in range(pb):
        r = (j * pb + pp) * 2
        ws = jnp.clip(r - edge, 0, rows - WIN_ROWS)
        var = jnp.where(r < edge, r // 2, jnp.where(r >= rows - edge, (r - (rows - edge)) // 2 + 3, 2))
        k0 = pl.multiple_of(ws * GRID_W, LANES)
        kwin = k_ref[pl.ds(k0, WIN_KEYS), :]
        vtw = vt_ref[:, pl.ds(k0, WIN_KEYS)]
        qt = qt_ref[:, pp * LANES:(pp + 1) * LANES]
        outs = []
        for jp in range(ATT_HEADS // 2):
            qpair = qt[jp * LANES:(jp + 1) * LANES, :]
            qboth = jnp.concatenate([jnp.where(low, qpair, zero), jnp.where(low, zero, qpair)], axis=1)
            s2 = _dot(kwin[:, jp * LANES:(jp + 1) * LANES], qboth)
            for hh in range(2):
                h = 2 * jp + hh
                s = s2[:, hh * LANES:(hh + 1) * LANES] + bias_ref[var, h]
                m = jnp.max(s, axis=0, keepdims=True)
                p = jnp.exp(s - m)
                den = jnp.sum(p, axis=0, keepdims=True)
                outs.append(_dot(vtw[h * ATT_HDIM:(h + 1) * ATT_HDIM, :], p.astype(BF16)) / den)
        ot = jnp.concatenate(outs, axis=0)
        o_ref[pp * LANES:(pp + 1) * LANES, :] = ot.T.astype(o_ref.dtype)


def _attention(qt, k, vt, bias_tab, tok0, n_seq, seq_len):
    rows = seq_len // GRID_W
    pb = ATT_PAIRS_PER_STEP
    assert rows >= WIN_ROWS and rows % (2 * pb) == 0 and tok0 % seq_len == 0 and 2 * GRID_W == LANES
    steps = rows // (2 * pb)
    s0 = tok0 // seq_len
    qb0 = tok0 // (pb * LANES)
    return pl.pallas_call(
        functools.partial(_att_kernel, rows=rows, pb=pb),
        grid=(n_seq, steps),
        in_specs=[pl.BlockSpec((ATT_W, pb * LANES), lambda b, j: (0, qb0 + b * steps + j)),
                  pl.BlockSpec((seq_len, ATT_W), lambda b, j: (s0 + b, 0)),
                  pl.BlockSpec((ATT_W, seq_len), lambda b, j: (0, s0 + b)),
                  pl.BlockSpec(bias_tab.shape, lambda b, j: (0, 0, 0, 0))],
        out_specs=pl.BlockSpec((pb * LANES, ATT_W), lambda b, j: (b * steps + j, 0)),
        out_shape=jax.ShapeDtypeStruct((n_seq * seq_len, ATT_W), BF16),
        compiler_params=_cparams(("parallel", "parallel")), name="nbr_attention",
    )(qt, k, vt, bias_tab)


def _layer_norm(x, g, b):
    mu = jnp.mean(x, axis=-1, keepdims=True)
    xc = x - mu
    var = jnp.mean(xc * xc, axis=-1, keepdims=True)
    return xc * lax.rsqrt(var + LN_EPS) * g + b


def _outproj_kernel(h_ref, pool_ref, yf_ref, yb_ref, xs_ref, z_ref, att_ref, wo_ref, dskip_ref, ng_ref,
                    g_ref, b_ref, wr_ref, br_ref, h1_ref, h1b_ref, idx_ref, rank_ref, gate_ref, cnt_ref, *, alpha):
    @pl.when(pl.program_id(0) == 0)
    def _():
        cnt_ref[...] = jnp.zeros(cnt_ref.shape, F32)

    y = yf_ref[...] + yb_ref[...] + xs_ref[...] * dskip_ref[...]
    z = z_ref[...]
    y = y * (z * jax.nn.sigmoid(z))
    parts = []
    for g in range(SSD_GROUPS):
        yg = y[:, g * SSD_GW:(g + 1) * SSD_GW]
        parts.append(yg * lax.rsqrt(jnp.mean(yg * yg, axis=-1, keepdims=True) + RMS_EPS))
    ssd = (jnp.concatenate(parts, axis=1) * ng_ref[...]).astype(BF16)
    m = (_dot(pool_ref[...].astype(BF16), wo_ref[0:POOL_W, :])
         + _dot(ssd, wo_ref[POOL_W:POOL_W + SSD_INNER, :])
         + _dot(att_ref[...], wo_ref[POOL_W + SSD_INNER:, :]))
    h1 = _layer_norm(alpha * h_ref[...] + m, g_ref[...], b_ref[...])
    h1_ref[...] = h1
    h1_hi = h1.astype(BF16)
    h1b_ref[...] = h1_hi
    h1_lo = (h1 - h1_hi.astype(F32)).astype(BF16)

    hh = _dot_nt(wr_ref[...], h1_hi)
    logits = hh[:N_EXPERTS] + hh[N_EXPERTS:] + _dot_nt(wr_ref[0:N_EXPERTS, :], h1_lo) + br_ref[...]
    eidx = lax.broadcasted_iota(jnp.int32, logits.shape, 0)
    vals, idxs = [], []
    cur = logits
    for _ in range(TOP_K):
        mx = jnp.max(cur, axis=0, keepdims=True)
        ix = jnp.min(jnp.where(cur == mx, eidx, N_EXPERTS), axis=0, keepdims=True)
        vals.append(mx)
        idxs.append(ix)
        cur = jnp.where(eidx == ix, -jnp.inf, cur)
    es = [jnp.exp(vv - vals[0]) for vv in vals]
    den = es[0] + es[1] + es[2] + es[3]
    tm = logits.shape[1]
    idx_ref[...] = jnp.concatenate(idxs + [jnp.zeros((8 - TOP_K, tm), jnp.int32)], axis=0)

    ti = lax.broadcasted_iota(jnp.int32, (LANES, LANES), 0)
    tj = lax.broadcasted_iota(jnp.int32, (LANES, LANES), 1)
    before = (ti < tj).astype(BF16)
    onehots = [(eidx == ix).astype(F32) for ix in idxs]
    sel = onehots[0] + onehots[1] + onehots[2] + onehots[3]
    base = cnt_ref[...]
    ranks = [[] for _ in range(TOP_K)]
    for q in range(tm // LANES):
        ls = slice(q * LANES, (q + 1) * LANES)
        prefix = _dot(sel[:, ls].astype(BF16), before) + base
        for k in range(TOP_K):
            ranks[k].append(jnp.sum(onehots[k][:, ls] * prefix, axis=0, keepdims=True))
        base = base + jnp.sum(sel[:, ls], axis=1, keepdims=True)
    cnt_ref[...] = base
    rank_rows = [jnp.concatenate(r, axis=1) for r in ranks] + [jnp.zeros((8 - TOP_K, tm), F32)]
    rank_ref[...] = jnp.concatenate(rank_rows, axis=0).astype(jnp.int32)

    gates_t = jnp.concatenate([e / den for e in es] + [jnp.zeros((LANES - TOP_K, tm), F32)], axis=0)
    gate_ref[...] = gates_t.T


def _out_proj_router(h, pool_out, y_f, y_b, xs, z, att, w_out, dskip, norm_g, ln_g, ln_b, w_router_t, b_router, alpha):
    t = h.shape[0]
    tm = TM_OUT
    row = lambda w: pl.BlockSpec((tm, w), lambda i: (i, 0))
    full = lambda a: pl.BlockSpec(a.shape, lambda i: (0,) * a.ndim)
    consts = (w_out, dskip, norm_g, ln_g, ln_b, w_router_t, b_router)
    return pl.pallas_call(
        functools.partial(_outproj_kernel, alpha=alpha),
        grid=(t // tm,),
        in_specs=[row(D_MODEL), row(POOL_W), row(SSD_INNER), row(SSD_INNER), row(SSD_INNER), row(SSD_INNER),
                  row(ATT_W)] + [full(a) for a in consts],
        out_specs=[row(D_MODEL), row(D_MODEL), pl.BlockSpec((8, tm), lambda i: (0, i)),
                   pl.BlockSpec((8, tm), lambda i: (0, i)), row(LANES),
                   pl.BlockSpec((N_EXPERTS, LANES), lambda i: (0, 0))],
        out_shape=[jax.ShapeDtypeStruct((t, D_MODEL), F32), jax.ShapeDtypeStruct((t, D_MODEL), BF16),
                   jax.ShapeDtypeStruct((8, t), jnp.int32), jax.ShapeDtypeStruct((8, t), jnp.int32),
                   jax.ShapeDtypeStruct((t, LANES), F32), jax.ShapeDtypeStruct((N_EXPERTS, LANES), F32)],
        compiler_params=_cparams(("arbitrary",)), name="out_proj_router",
    )(h, pool_out, y_f, y_b, xs, z, att, *consts)


def _expert_kernel(blk_ref, e_ref, lo_ref, hi_ref, first_ref, newe_ref, x_ref, w1_ref, b1g_ref, b1l_ref,
                   w2_ref, b2_ref, perm_ref, o_ref, w1g_s, w1l_s, w2_s):
    w = pl.program_id(0)
    lo = lo_ref[w]
    hi = hi_ref[w]

    @pl.when(newe_ref[w] == 1)
    def _():
        for j in range(D_FF // LANES):
            chunk = w1_ref[:, 2 * LANES * j:2 * LANES * (j + 1)].astype(BF16)
            r = _dot(chunk, perm_ref[...])
            w1g_s[:, LANES * j:LANES * (j + 1)] = r[:, :LANES].astype(BF16)
            w1l_s[:, LANES * j:LANES * (j + 1)] = r[:, LANES:].astype(BF16)
        w2_s[...] = w2_ref[...].astype(BF16)

    @pl.when(hi > lo)
    def _():
        x = x_ref[...]
        acc = jnp.zeros(o_ref.shape, F32) + b2_ref[...]
        for c in range(0, D_FF, FF_CHUNK):
            hg = _dot(x, w1g_s[:, c:c + FF_CHUNK]) + b1g_ref[:, c:c + FF_CHUNK]
            hl = _dot(x, w1l_s[:, c:c + FF_CHUNK]) + b1l_ref[:, c:c + FF_CHUNK]
            hg = jnp.minimum(hg, SWIGLU_LIMIT)
            hl = jnp.clip(hl, -SWIGLU_LIMIT, SWIGLU_LIMIT)
            act = hg * jax.nn.sigmoid(SWIGLU_ALPHA * hg) * (hl + 1.0)
            acc = acc + _dot(act.astype(BF16), w2_s[c:c + FF_CHUNK, :])
        rows = blk_ref[w] * MOE_BM + lax.broadcasted_iota(jnp.int32, (MOE_BM, 1), 0)
        mine = (rows >= lo) & (rows < hi)
        y = acc.astype(o_ref.dtype)

        @pl.when(first_ref[w] == 1)
        def _():
            o_ref[...] = jnp.where(mine, y, jnp.zeros_like(y))

        @pl.when(first_ref[w] == 0)
        def _():
            o_ref[...] = jnp.where(mine, y, o_ref[...])


def _experts(x_sorted, meta, layer, w1, b1g, b1l, w2, b2):
    n_rows = x_sorted.shape[0]
    bm = MOE_BM
    n_items = meta[0].shape[0]
    o = np.arange(2 * LANES)[None, :]
    c = np.arange(2 * LANES)[:, None]
    perm = jnp.asarray(np.where(o < LANES, c == 2 * o, c == 2 * (o - LANES) + 1), BF16)
    wspec = lambda r, c: pl.BlockSpec((None, None, r, c), lambda w, blk, e, *_: (layer, e[w], 0, 0))
    bspec = lambda c: pl.BlockSpec((None, 1, c), lambda w, blk, e, *_: (e[w], 0, 0))
    grid_spec = pltpu.PrefetchScalarGridSpec(
        num_scalar_prefetch=6, grid=(n_items,),
        in_specs=[pl.BlockSpec((bm, D_MODEL), lambda w, blk, *_: (blk[w], 0)),
                  wspec(D_MODEL, 2 * D_FF), bspec(D_FF), bspec(D_FF), wspec(D_FF, D_MODEL), bspec(D_MODEL),
                  pl.BlockSpec(perm.shape, lambda w, *_: (0, 0))],
        out_specs=pl.BlockSpec((bm, D_MODEL), lambda w, blk, *_: (blk[w], 0)),
        scratch_shapes=[pltpu.VMEM((D_MODEL, D_FF), BF16), pltpu.VMEM((D_MODEL, D_FF), BF16),
                        pltpu.VMEM((D_FF, D_MODEL), BF16)])
    return pl.pallas_call(
        _expert_kernel, grid_spec=grid_spec,
        out_shape=jax.ShapeDtypeStruct((n_rows, D_MODEL), BF16),
        compiler_params=_cparams(("arbitrary",)), name="moe_experts",
    )(*meta, x_sorted, w1, b1g, b1l, w2, b2, perm)


def _combine_kernel(h_ref, y_ref, gate_ref, g_ref, b_ref, o_ref, *, alpha):
    gates = gate_ref[...]
    f = y_ref[0].astype(F32) * gates[:, 0:1]
    for k in range(1, TOP_K):
        f = f + y_ref[k].astype(F32) * gates[:, k:k + 1]
    o_ref[...] = _layer_norm(alpha * h_ref[...] + f, g_ref[...], b_ref[...])


def _combine_ln(h1, y_tok, gate_col, ln_g, ln_b, alpha, tok0, n_tok):
    tm = TM_OUT
    assert tok0 % tm == 0 and n_tok % tm == 0
    b0 = tok0 // tm
    return pl.pallas_call(
        functools.partial(_combine_kernel, alpha=alpha),
        grid=(n_tok // tm,),
        in_specs=[pl.BlockSpec((tm, D_MODEL), lambda i: (b0 + i, 0)),
                  pl.BlockSpec((TOP_K, tm, D_MODEL), lambda i: (0, b0 + i, 0)),
                  pl.BlockSpec((tm, LANES), lambda i: (b0 + i, 0)),
                  pl.BlockSpec((1, D_MODEL), lambda i: (0, 0)),
                  pl.BlockSpec((1, D_MODEL), lambda i: (0, 0))],
        out_specs=pl.BlockSpec((tm, D_MODEL), lambda i: (i, 0)),
        out_shape=jax.ShapeDtypeStruct((n_tok, D_MODEL), F32),
        compiler_params=_cparams(("parallel",)), name="combine_ln",
    )(h1, y_tok, gate_col, ln_g, ln_b)


def _routing(idx_t, rank_t, counts):
    t = idx_t.shape[1]
    n_asg = TOP_K * t
    bm = MOE_BM
    n_blocks = n_asg // bm
    n_items = n_blocks + N_EXPERTS - 1
    i32 = jnp.int32
    idx = idx_t[:TOP_K]
    tok = jnp.arange(t, dtype=i32)[None, :]
    skeys = lax.sort((idx * t + tok).reshape(-1))
    src_tok = skeys % t
    ends = jnp.cumsum(counts.astype(i32))
    starts = ends - counts.astype(i32)
    experts = jnp.arange(N_EXPERTS, dtype=i32)[:, None, None]
    pos = (jnp.sum(jnp.where(idx[None] == experts, starts[:, None, None], 0), axis=0) + rank_t[:TOP_K]).reshape(-1)
    first_tile = starts // bm
    ntiles = jnp.where(ends > starts, (ends + bm - 1) // bm - first_tile, 0)
    cum = jnp.cumsum(ntiles)
    base = cum - ntiles
    total = cum[-1]
    w = jnp.arange(n_items, dtype=i32)
    valid = w < total
    e_w = jnp.minimum(jnp.sum((cum[None, :] <= w[:, None]).astype(i32), axis=1), N_EXPERTS - 1)
    e_last = e_w[jnp.maximum(total - 1, 0)]
    e_w = jnp.where(valid, e_w, e_last)
    blk = jnp.where(valid, first_tile[e_w] + (w - base[e_w]), n_blocks - 1).astype(i32)
    lo = jnp.where(valid, jnp.maximum(starts[e_w], blk * bm), 0).astype(i32)
    hi = jnp.where(valid, jnp.minimum(ends[e_w], (blk + 1) * bm), 0).astype(i32)
    one = jnp.ones((1,), i32)
    first = jnp.concatenate([one, (blk[1:] != blk[:-1]).astype(i32)])
    new_e = jnp.concatenate([one, (e_w[1:] != e_w[:-1]).astype(i32)])
    return src_tok, pos, (blk, e_w, lo, hi, first, new_e)


def _prep_layer(i, w_in, conv_w, conv_b, a_log, dt_bias, w_pool, pool_scale, rpb, w_out, w_router, b_router,
                w1, b1, w2, b2, d_skip, ssd_norm_g, ln1_g, ln1_b, ln2_g, ln2_b):
    c_x = POOL_W + SSD_INNER + SSD_XBC
    c_dt = c_x + DT_COLS
    w = w_in[i]
    w_dt = w[:, c_x:c_dt]
    w_q, w_k, w_v = (w[:, c_dt + j * ATT_W:c_dt + (j + 1) * ATT_W] for j in range(3))
    pad = LANES - DT_REP * DT_COLS
    w_all = jnp.concatenate([w[:, :c_x], w_k] + [w_dt] * DT_REP + [jnp.zeros((D_MODEL, pad), F32)],
                            axis=1).astype(BF16)
    w_t = jnp.concatenate([w_q, w_v, w_dt], axis=1).T.astype(BF16)
    packed_row = lambda v: jnp.concatenate([v.reshape(1, DT_COLS)] * DT_REP + [jnp.zeros((1, pad), F32)], axis=1)
    w_bd = jnp.zeros((POOL_W, POOL_W), F32)
    for g in range(POOL_GROUPS):
        sl = slice(g * POOL_GDIM, (g + 1) * POOL_GDIM)
        w_bd = w_bd.at[sl, sl].set(w_pool[i, g])
    conv_w_pad = jnp.concatenate([conv_w[i], jnp.zeros((HALO - SSD_CONV, SSD_XBC), F32)], axis=0)
    return dict(
        layer=i, w1=w1, w2=w2,
        w_all=w_all, w_t=w_t, w_bd=w_bd.astype(BF16), pool_scale=pool_scale[i].reshape(1, POOL_W),
        conv_w=conv_w_pad, conv_b=conv_b[i].reshape(1, SSD_XBC),
        dt_bias_row=packed_row(dt_bias[i]), dt_bias_col=dt_bias[i].reshape(DT_COLS, 1),
        alog_row=packed_row(a_log[i]), alog_col=a_log[i].reshape(DT_COLS, 1),
        bias_tab=_att_bias_table(rpb[i]),
        w_out=w_out[i].astype(BF16), dskip=jnp.repeat(d_skip[i], SSD_HDIM).reshape(1, SSD_INNER),
        norm_g=ssd_norm_g[i].reshape(1, SSD_INNER), ln1_g=ln1_g[i].reshape(1, D_MODEL), ln1_b=ln1_b[i].reshape(1, D_MODEL),
        w_router_t=jnp.concatenate(_split3(w_router[i].T)[:2], axis=0), b_router=b_router[i].reshape(N_EXPERTS, 1),
        b1g=b1[i, :, 0::2].reshape(N_EXPERTS, 1, D_FF), b1l=b1[i, :, 1::2].reshape(N_EXPERTS, 1, D_FF),
        b2=b2[i].reshape(N_EXPERTS, 1, D_MODEL),
        ln2_g=ln2_g[i].reshape(1, D_MODEL), ln2_b=ln2_b[i].reshape(1, D_MODEL))


def _encoder_layer(h, p, seq_groups, alpha, split_out):
    t = h.shape[0]
    seq_lens = tuple(s for tok0, n, s in seq_groups for _ in range(n))
    u, z, xbc, k, dt, qt, vt, dtt = _in_proj(h, p['w_all'], p['w_t'])
    pool_out = _pool_mixer(u, p['w_bd'], p['pool_scale'], seq_lens)
    xs, xcf, xcb, c, bt = _conv_silu(xbc, dt, p['conv_w'], p['conv_b'], p['dt_bias_row'], seq_lens)
    y_f, y_b = _ssd_scan(xcf, xcb, c, bt, dt, dtt, p['dt_bias_row'], p['dt_bias_col'], p['alog_row'], p['alog_col'],
                         seq_lens)
    att = jnp.concatenate([_attention(qt, k, vt, p['bias_tab'], tok0, n, s) for tok0, n, s in seq_groups], axis=0)
    h1, h1b, idx_t, rank_t, gate_col, cnt = _out_proj_router(
        h, pool_out, y_f, y_b, xs, z, att, p['w_out'], p['dskip'], p['norm_g'], p['ln1_g'], p['ln1_b'],
        p['w_router_t'], p['b_router'], alpha)
    src_tok, pos, meta = _routing(idx_t, rank_t, cnt[:, 0])
    x_sorted = jnp.take(h1b, src_tok, axis=0, mode='clip')
    y_sorted = _experts(x_sorted, meta, p['layer'], p['w1'], p['b1g'], p['b1l'], p['w2'], p['b2'])
    y_tok = jnp.take(y_sorted, pos, axis=0, mode='clip').reshape(TOP_K, t, D_MODEL)
    ranges = [(tok0, n * s) for tok0, n, s in seq_groups] if split_out else [(0, t)]
    return [_combine_ln(h1, y_tok, gate_col, p['ln2_g'], p['ln2_b'], alpha, tok0, n_tok) for tok0, n_tok in ranges]


def kernel(x_prompt, x_sample, w_in, conv_w, conv_b, a_log, dt_bias, d_skip, ssd_norm_g, w_pool, pool_scale, rpb, w_out,
           ln1_g, ln1_b, w_router, b_router, w1, b1, w2, b2, ln2_g, ln2_b):
    depth = w_in.shape[0]
    alpha = (2 * depth) ** 0.25
    bp, sp, _ = x_prompt.shape
    bs, ss, _ = x_sample.shape
    seq_groups = ((0, bp, sp), (bp * sp, bs, ss))
    h = jnp.concatenate([x_prompt.reshape(bp * sp, D_MODEL), x_sample.reshape(bs * ss, D_MODEL)], axis=0)
    for i in range(depth):
        p = _prep_layer(i, w_in, conv_w, conv_b, a_log, dt_bias, w_pool, pool_scale, rpb, w_out, w_router, b_router,
                        w1, b1, w2, b2, d_skip, ssd_norm_g, ln1_g, ln1_b, ln2_g, ln2_b)
        outs = _encoder_layer(h, p, seq_groups, alpha, split_out=(i == depth - 1))
        h = outs[0]
    return (outs[0].reshape(bp, sp, D_MODEL), outs[1].reshape(bs, ss, D_MODEL))
```

```python
import functools

import numpy as np
import jax
import jax.numpy as jnp
from jax import lax
from jax.experimental import pallas as pl
from jax.experimental.pallas import tpu as pltpu

F32 = jnp.float32
BF16 = jnp.bfloat16

D_MODEL = 1024
POOL_GROUPS = 4
POOL_GDIM = 64
POOL_W = POOL_GROUPS * POOL_GDIM
POOL_WINDOWS = (2, 4, 8, 16)
SSD_HEADS = 8
SSD_HDIM = 64
SSD_INNER = SSD_HEADS * SSD_HDIM
SSD_GROUPS = 2
SSD_STATE = 128
SSD_CONV = 5
SSD_CHUNK = 128
SSD_BC = 2 * SSD_GROUPS * SSD_STATE
SSD_XBC = SSD_INNER + SSD_BC
SSD_GW = SSD_INNER // SSD_GROUPS
ATT_HEADS = 4
ATT_HDIM = 64
ATT_W = ATT_HEADS * ATT_HDIM
GRID_W = 64
NA_ROWS = 8
NA_COLS = 16
N_EXPERTS = 32
TOP_K = 4
D_FF = D_MODEL
SWIGLU_LIMIT = 7.0
SWIGLU_ALPHA = 1.702
LN_EPS = 1e-5
RMS_EPS = 1e-5

LANES = 128
DT_COLS = 2 * SSD_HEADS
DT_REP = 3
HALO = 8
NEG_BIG = -1e30
WIN_ROWS = NA_ROWS + 2
WIN_KEYS = WIN_ROWS * GRID_W

TM_PROJ = 512
TM_OUT = 512
SSD_CHUNKS_PER_STEP = 4
TS_SEQ = 512
ATT_PAIRS_PER_STEP = 4
MOE_BM = 512
FF_CHUNK = 512
VMEM_LIMIT = 56 * 1024 * 1024


def _cparams(sem):
    return pltpu.CompilerParams(dimension_semantics=sem, vmem_limit_bytes=VMEM_LIMIT)


def _dot(a, b):
    return jnp.dot(a, b, preferred_element_type=F32)


def _dot_nt(a, b, precision=None):
    return lax.dot_general(a, b, (((1,), (1,)), ((), ())), preferred_element_type=F32, precision=precision)


def _softplus(v):
    return jnp.maximum(v, 0.0) + jnp.log1p(jnp.exp(-jnp.abs(v)))


def _split3(v):
    hi = v.astype(BF16)
    r1 = v - hi.astype(F32)
    mid = r1.astype(BF16)
    lo = (r1 - mid.astype(F32)).astype(BF16)
    return hi, mid, lo


def _pack3(v):
    hi, mid, lo = _split3(v)
    lane = lax.broadcasted_iota(jnp.int32, v.shape, 1)
    zero = jnp.zeros(v.shape, BF16)
    return jnp.where(lane < DT_COLS, hi, jnp.where(lane < 2 * DT_COLS, mid, jnp.where(lane < 3 * DT_COLS, lo, zero)))


def _expand_matrix(off, width):
    r = np.arange(LANES)[:, None]
    c = np.arange(SSD_HEADS * width)[None, :]
    m = (r < DT_REP * DT_COLS) & ((r % DT_COLS) == off + c // width)
    return jnp.asarray(m, BF16)


def _token_tile_specs(srcs, tm):
    specs, bounds, first = [], [], 0
    for a in srcs:
        n = a.shape[0] // tm
        assert a.shape[0] % tm == 0
        specs.append(pl.BlockSpec((tm, a.shape[1]), lambda i, first=first, n=n: (jnp.clip(i - first, 0, n - 1), 0)))
        first += n
        bounds.append(first)
    return specs, tuple(bounds)


def _pick_source(refs, bounds):
    x = refs[-1][...]
    for ref, end in zip(reversed(refs[:-1]), reversed(bounds[:-1])):
        x = jnp.where(pl.program_id(0) < end, ref[...], x)
    return x


def _inproj_kernel(*refs, bounds):
    n_src = len(bounds)
    w_ref, wt_ref, u_ref, z_ref, xbc_ref, k_ref, dt_ref, qt_ref, vt_ref, dtt_ref = refs[n_src:]
    xb = _pick_source(refs[:n_src], bounds).astype(BF16)
    c0 = 0
    for ref, width in ((u_ref, POOL_W), (z_ref, SSD_INNER), (xbc_ref, SSD_XBC), (k_ref, ATT_W), (dt_ref, LANES)):
        ref[...] = _dot(xb, w_ref[:, c0:c0 + width]).astype(ref.dtype)
        c0 += width
    tr = _dot_nt(wt_ref[...], xb)
    qt_ref[...] = (tr[:ATT_W] * ATT_HDIM ** -0.5).astype(BF16)
    vt_ref[...] = tr[ATT_W:2 * ATT_W].astype(BF16)
    dtt_ref[...] = tr[2 * ATT_W:]


def _in_proj(srcs, w_all, w_t):
    t = sum(a.shape[0] for a in srcs)
    tm = TM_PROJ
    row = lambda w: pl.BlockSpec((tm, w), lambda i: (i, 0))
    col = lambda r: pl.BlockSpec((r, tm), lambda i: (0, i))
    full = lambda a: pl.BlockSpec(a.shape, lambda i: (0,) * a.ndim)
    src_specs, bounds = _token_tile_specs(srcs, tm)
    return pl.pallas_call(
        functools.partial(_inproj_kernel, bounds=bounds),
        grid=(t // tm,),
        in_specs=src_specs + [full(w_all), full(w_t)],
        out_specs=[row(POOL_W), row(SSD_INNER), row(SSD_XBC), row(ATT_W), row(LANES),
                   col(ATT_W), col(ATT_W), col(DT_COLS)],
        out_shape=[jax.ShapeDtypeStruct((t, POOL_W), F32), jax.ShapeDtypeStruct((t, SSD_INNER), F32),
                   jax.ShapeDtypeStruct((t, SSD_XBC), F32), jax.ShapeDtypeStruct((t, ATT_W), BF16),
                   jax.ShapeDtypeStruct((t, LANES), F32), jax.ShapeDtypeStruct((ATT_W, t), BF16),
                   jax.ShapeDtypeStruct((ATT_W, t), BF16), jax.ShapeDtypeStruct((DT_COLS, t), F32)],
        compiler_params=_cparams(("parallel",)),
        name="in_proj",
    )(*srcs, w_all, w_t)


def _seq_tables(seq_lens, tile):
    pos, length = [], []
    for s in seq_lens:
        assert s % tile == 0
        for p in range(0, s, tile):
            pos.append(p)
            length.append(s)
    return np.asarray(pos, np.int32), np.asarray(length, np.int32)


def _halo_specs(ts, width, n_tok):
    hb = ts // HALO
    last = n_tok // HALO - 1
    cur = pl.BlockSpec((ts, width), lambda i, *_: (i, 0))
    prev = pl.BlockSpec((HALO, width), lambda i, *_: (jnp.maximum(i * hb - 1, 0), 0))
    nxt = pl.BlockSpec((HALO, width), lambda i, *_: (jnp.minimum((i + 1) * hb, last), 0))
    return prev, cur, nxt


def _extended(prev_ref, cur_ref, next_ref, pos0, slen, ts):
    prev = jnp.where(pos0 > 0, prev_ref[...], 0.0)
    nxt = jnp.where(pos0 + ts < slen, next_ref[...], 0.0)
    return jnp.concatenate([prev, cur_ref[...], nxt], axis=0)


def _rows_from(ext, k, ts):
    n = ext.shape[0]
    shift = (n - (HALO + k)) % n
    rolled = ext if shift == 0 else pltpu.roll(ext, shift, 0)
    return rolled[:ts]


def _pool_kernel(pos_ref, len_ref, prev_ref, cur_ref, next_ref, w_ref, scale_ref, o_ref, *, ts):
    i = pl.program_id(0)
    pos0 = pos_ref[i]
    slen = len_ref[i]
    ext = _extended(prev_ref, cur_ref, next_ref, pos0, slen, ts)
    n = ext.shape[0]

    def pair_sum(a, k):
        return a + pltpu.roll(a, n - k, 0)

    a2 = pair_sum(ext, 1)
    a4 = pair_sum(a2, 2)
    a8 = pair_sum(a4, 4)
    a16 = pair_sum(a8, 8)
    sums = [_rows_from(a, -w // 2, ts) for a, w in zip((a2, a4, a8, a16), POOL_WINDOWS)]

    lane = lax.broadcasted_iota(jnp.int32, (ts, POOL_W), 1)
    grp = lane // POOL_GDIM
    wsum = jnp.where(grp == 0, sums[0], jnp.where(grp == 1, sums[1], jnp.where(grp == 2, sums[2], sums[3])))
    half = jnp.where(grp == 0, 1, jnp.where(grp == 1, 2, jnp.where(grp == 2, 4, 8)))
    tpos = pos0 + lax.broadcasted_iota(jnp.int32, (ts, POOL_W), 0)
    cnt = (jnp.minimum(tpos + half, slen) - jnp.maximum(tpos - half, 0)).astype(F32)
    pooled = wsum / cnt - cur_ref[...]
    o_ref[...] = _dot(pooled.astype(BF16), w_ref[...]) * scale_ref[...]


def _pool_mixer(u, w_bd, scale, seq_lens):
    t = u.shape[0]
    ts = TS_SEQ
    pos, length = _seq_tables(seq_lens, ts)
    prev, cur, nxt = _halo_specs(ts, POOL_W, t)
    grid_spec = pltpu.PrefetchScalarGridSpec(
        num_scalar_prefetch=2, grid=(t // ts,),
        in_specs=[prev, cur, nxt,
                  pl.BlockSpec((POOL_W, POOL_W), lambda i, *_: (0, 0)),
                  pl.BlockSpec((1, POOL_W), lambda i, *_: (0, 0))],
        out_specs=pl.BlockSpec((ts, POOL_W), lambda i, *_: (i, 0)))
    return pl.pallas_call(
        functools.partial(_pool_kernel, ts=ts), grid_spec=grid_spec,
        out_shape=jax.ShapeDtypeStruct((t, POOL_W), F32),
        compiler_params=_cparams(("parallel",)), name="pool_mixer",
    )(pos, length, u, u, u, w_bd, scale)


def _conv_kernel(pos_ref, len_ref, prev_ref, cur_ref, next_ref, dt_ref, w_ref, b_ref, dtb_ref, ef_ref, eb_ref,
                 xs_ref, xcf_ref, xcb_ref, c_ref, bt_ref, *, ts):
    i = pl.program_id(0)
    ext = _extended(prev_ref, cur_ref, next_ref, pos_ref[i], len_ref[i], ts)
    acc = jnp.zeros((ts, SSD_XBC), F32) + b_ref[...]
    for j in range(SSD_CONV):
        acc = acc + _rows_from(ext, j - SSD_CONV // 2, ts) * w_ref[j:j + 1, :]
    act = acc * jax.nn.sigmoid(acc)
    xs = act[:, :SSD_INNER]
    xs_ref[...] = xs
    dt_pk = _pack3(_softplus(dt_ref[...] + dtb_ref[...]))
    xcf_ref[...] = (xs * _dot(dt_pk, ef_ref[...])).astype(BF16)
    xcb_ref[...] = (xs * _dot(dt_pk, eb_ref[...])).astype(BF16)
    nb = SSD_GROUPS * SSD_STATE
    bt_ref[...] = act[:, SSD_INNER:SSD_INNER + nb].T.astype(BF16)
    c_ref[...] = act[:, SSD_INNER + nb:].astype(BF16)


def _conv_silu(xbc, dt, conv_w, conv_b, dt_bias_row, seq_lens):
    t = xbc.shape[0]
    ts = TS_SEQ
    pos, length = _seq_tables(seq_lens, ts)
    prev, cur, nxt = _halo_specs(ts, SSD_XBC, t)
    e_f = _expand_matrix(0, SSD_HDIM)
    e_b = _expand_matrix(SSD_HEADS, SSD_HDIM)
    nb = SSD_GROUPS * SSD_STATE
    const = lambda a: pl.BlockSpec(a.shape, lambda i, *_: (0, 0))
    row = lambda w: pl.BlockSpec((ts, w), lambda i, *_: (i, 0))
    grid_spec = pltpu.PrefetchScalarGridSpec(
        num_scalar_prefetch=2, grid=(t // ts,),
        in_specs=[prev, cur, nxt, row(LANES), const(conv_w), const(conv_b), const(dt_bias_row), const(e_f), const(e_b)],
        out_specs=[row(SSD_INNER), row(SSD_INNER), row(SSD_INNER), row(nb),
                   pl.BlockSpec((nb, ts), lambda i, *_: (0, i))])
    return pl.pallas_call(
        functools.partial(_conv_kernel, ts=ts), grid_spec=grid_spec,
        out_shape=[jax.ShapeDtypeStruct((t, SSD_INNER), F32), jax.ShapeDtypeStruct((t, SSD_INNER), BF16),
                   jax.ShapeDtypeStruct((t, SSD_INNER), BF16), jax.ShapeDtypeStruct((t, nb), BF16),
                   jax.ShapeDtypeStruct((nb, t), BF16)],
        compiler_params=_cparams(("parallel",)), name="conv_silu",
    )(pos, length, xbc, xbc, xbc, dt, conv_w, conv_b, dt_bias_row, e_f, e_b)


def _ssd_chunk(xc, c, bt, dt_full, dtt, aneg_full, aneg_col, e128, state_ref, reset, reverse, off):
    L = SSD_CHUNK
    ii = lax.broadcasted_iota(jnp.int32, (L, L), 0)
    jj = lax.broadcasted_iota(jnp.int32, (L, L), 1)
    tri = (jj >= ii) if reverse else (jj <= ii)
    tri_t = (ii >= jj) if reverse else (ii <= jj)
    tri_b = tri.astype(BF16)
    tri_tb = tri_t.astype(BF16)
    acs = sum(_dot(tri_b, part) for part in _split3(dt_full * aneg_full))
    acs_row = sum(_dot(part, tri_tb) for part in _split3(dtt * aneg_col))
    edge = 0 if reverse else L - 1
    acs_b = _dot(_pack3(acs), e128)
    lane = lax.broadcasted_iota(jnp.int32, (L, LANES), 1)
    low = lane < SSD_HDIM
    acs_e = jnp.concatenate(
        [jnp.where(low, acs_b[:, (2 * j) * LANES:(2 * j + 1) * LANES], acs_b[:, (2 * j + 1) * LANES:(2 * j + 2) * LANES])
         for j in range(SSD_HEADS // 2)], axis=1)
    tot_e = acs_e[edge:edge + 1, :]
    exp_acs = jnp.exp(acs_e)
    to_end = jnp.exp(tot_e - acs_e)
    chunk_decay = jnp.exp(tot_e)
    xw = (xc.astype(F32) * to_end).astype(BF16)
    zero_b = jnp.zeros((L, LANES), BF16)

    ys = []
    for g in range(SSD_GROUPS):
        gs = slice(g * SSD_GW, (g + 1) * SSD_GW)
        c_g = c[:, g * SSD_STATE:(g + 1) * SSD_STATE]
        bt_g = bt[g * SSD_STATE:(g + 1) * SSD_STATE, :]
        cb = _dot(c_g, bt_g)
        prev = state_ref[g] if reset is None else jnp.where(reset, 0.0, state_ref[g])
        y_off = _dot(c_g, prev.astype(BF16)) * exp_acs[:, gs]
        state_ref[g] = prev * chunk_decay[:, gs] + _dot(bt_g, xw[:, gs])
        for pair in range(SSD_GW // LANES):
            t0 = g * SSD_GW + pair * LANES
            tile = xc[:, t0:t0 + LANES]
            halves = (jnp.where(low, tile, zero_b), jnp.where(low, zero_b, tile))
            y_pair = y_off[:, pair * LANES:(pair + 1) * LANES]
            for s in range(2):
                h = t0 // SSD_HDIM + s
                seg = acs_b[:, h * LANES:(h + 1) * LANES] - acs_row[off + h:off + h + 1, :]
                m = (cb * jnp.exp(jnp.where(tri, seg, NEG_BIG))).astype(BF16)
                y_pair = y_pair + _dot(m, halves[s])
            ys.append(y_pair)
    return jnp.concatenate(ys, axis=1)


def _ssd_kernel(mir_ref, start_ref, xc_f, c_f, bt_f, dt_f, dtt_f, xc_b, c_b, bt_b, dt_b, dtt_b,
                bias_row, bias_col, alog_row, alog_col, ef_ref, eb_ref, yf_ref, yb_ref, st_f, st_b):
    step = pl.program_id(0)
    L = SSD_CHUNK
    first = start_ref[step] == 1
    aneg_full = -jnp.exp(alog_row[...])
    aneg_col = -jnp.exp(alog_col[...])
    dirs = ((xc_f, c_f, bt_f, dt_f, dtt_f, ef_ref, yf_ref, st_f, False, 0),
            (xc_b, c_b, bt_b, dt_b, dtt_b, eb_ref, yb_ref, st_b, True, SSD_HEADS))
    for n in range(SSD_CHUNKS_PER_STEP):
        for xc, c, bt, dt, dtt, e_ref, y_ref, st, reverse, off in dirs:
            sub = SSD_CHUNKS_PER_STEP - 1 - n if reverse else n
            rows = slice(sub * L, (sub + 1) * L)
            dtc = _softplus(dt[rows, :] + bias_row[...])
            dtr = _softplus(dtt[:, rows] + bias_col[...])
            reset = first if n == 0 else None
            y_ref[rows, :] = _ssd_chunk(xc[rows, :], c[rows, :], bt[:, rows], dtc, dtr, aneg_full, aneg_col, e_ref[...],
                                        st, reset, reverse, off)


def _ssd_tables(seq_lens, block):
    mirror, start = [], []
    c0 = 0
    for s in seq_lens:
        assert s % block == 0
        n = s // block
        for c in range(n):
            mirror.append(c0 + n - 1 - c)
            start.append(1 if c == 0 else 0)
        c0 += n
    return np.asarray(mirror, np.int32), np.asarray(start, np.int32)


def _ssd_scan(xcf, xcb, c, bt, dt, dtt, dt_bias_row, dt_bias_col, alog_row, alog_col, seq_lens):
    t = xcf.shape[0]
    L = SSD_CHUNK * SSD_CHUNKS_PER_STEP
    nb = SSD_GROUPS * SSD_STATE
    mirror, start = _ssd_tables(seq_lens, L)
    e_f = _expand_matrix(0, LANES)
    e_b = _expand_matrix(SSD_HEADS, LANES)
    fw = lambda w: pl.BlockSpec((L, w), lambda s, m, st: (s, 0))
    bw = lambda w: pl.BlockSpec((L, w), lambda s, m, st: (m[s], 0))
    fw_t = lambda r: pl.BlockSpec((r, L), lambda s, m, st: (0, s))
    bw_t = lambda r: pl.BlockSpec((r, L), lambda s, m, st: (0, m[s]))
    const = lambda a: pl.BlockSpec(a.shape, lambda s, m, st: (0, 0))
    consts = (dt_bias_row, dt_bias_col, alog_row, alog_col, e_f, e_b)
    grid_spec = pltpu.PrefetchScalarGridSpec(
        num_scalar_prefetch=2, grid=(t // L,),
        in_specs=[fw(SSD_INNER), fw(nb), fw_t(nb), fw(LANES), fw_t(DT_COLS),
                  bw(SSD_INNER), bw(nb), bw_t(nb), bw(LANES), bw_t(DT_COLS)] + [const(a) for a in consts],
        out_specs=[fw(SSD_INNER), bw(SSD_INNER)],
        scratch_shapes=[pltpu.VMEM((SSD_GROUPS, SSD_STATE, SSD_GW), F32),
                        pltpu.VMEM((SSD_GROUPS, SSD_STATE, SSD_GW), F32)])
    return pl.pallas_call(
        _ssd_kernel, grid_spec=grid_spec,
        out_shape=[jax.ShapeDtypeStruct((t, SSD_INNER), F32), jax.ShapeDtypeStruct((t, SSD_INNER), F32)],
        compiler_params=_cparams(("arbitrary",)), name="ssd_scan",
    )(mirror, start, xcf, c, bt, dt, dtt, xcb, c, bt, dt, dtt, *consts)


ATT_VARIANTS = 5
_ATT_RS_OFF = ((0, -1), (-2, -3), (-4, -4), (-4, -5), (-6, -7))


def _att_bias_table(rpb):
    kc = np.arange(GRID_W)[:, None]
    c = np.arange(GRID_W)[None, :]
    cs = np.clip(c - NA_COLS // 2, 0, GRID_W - NA_COLS)
    valid = (kc >= cs) & (kc < cs + NA_COLS)
    d = kc - c + NA_COLS - 1
    onehot = (d[None] == np.arange(2 * NA_COLS - 1)[:, None, None]) & valid[None]
    tt = jnp.einsum('hrd,dkc->hrkc', rpb.astype(F32), jnp.asarray(onehot, F32),
                    precision=jax.lax.Precision.HIGHEST)
    tt = jnp.where(valid, tt, NEG_BIG)
    neg = jnp.full((ATT_HEADS, GRID_W, GRID_W), NEG_BIG, F32)
    variants = []
    for var in range(ATT_VARIANTS):
        halves = []
        for dq in range(2):
            tiles = []
            for i in range(WIN_ROWS):
                delta = i - 2 * var - dq
                inside = _ATT_RS_OFF[var][dq] <= delta < _ATT_RS_OFF[var][dq] + NA_ROWS
                tiles.append(tt[:, delta + NA_ROWS - 1] if inside else neg)
            halves.append(jnp.concatenate(tiles, axis=1))
        variants.append(jnp.concatenate(halves, axis=2))
    return jnp.stack(variants)


def _att_kernel(qt_ref, k_ref, vt_ref, bias_ref, o_ref, *, rows, pb):
    j = pl.program_id(1)
    sub = lax.broadcasted_iota(jnp.int32, (LANES, LANES), 0)
    low = sub < ATT_HDIM
    zero = jnp.zeros((LANES, LANES), BF16)
    edge = NA_ROWS // 2
    for pp in range(pb):
        r = (j * pb + pp) * 2
        ws = jnp.clip(r - edge, 0, rows - WIN_ROWS)
        var = jnp.where(r < edge, r // 2, jnp.where(r >= rows - edge, (r - (rows - edge)) // 2 + 3, 2))
        k0 = pl.multiple_of(ws * GRID_W, LANES)
        kwin = k_ref[pl.ds(k0, WIN_KEYS), :]
        vtw = vt_ref[:, pl.ds(k0, WIN_KEYS)]
        qt = qt_ref[:, pp * LANES:(pp + 1) * LANES]
        outs = []
        for jp in range(ATT_HEADS // 2):
            qpair = qt[jp * LANES:(jp + 1) * LANES, :]
            qboth = jnp.concatenate([jnp.where(low, qpair, zero), jnp.where(low, zero, qpair)], axis=1)
            s2 = _dot(kwin[:, jp * LANES:(jp + 1) * LANES], qboth)
            for hh in range(2):
                h = 2 * jp + hh
                s = s2[:, hh * LANES:(hh + 1) * LANES] + bias_ref[var, h]
                m = jnp.max(s, axis=0, keepdims=True)
                p = jnp.exp(s - m)
                den = jnp.sum(p, axis=0, keepdims=True)
                outs.append(_dot(vtw[h * ATT_HDIM:(h + 1) * ATT_HDIM, :], p.astype(BF16)) / den)
        ot = jnp.concatenate(outs, axis=0)
        o_ref[pp * LANES:(pp + 1) * LANES, :] = ot.T.astype(o_ref.dtype)


def _attention(qt, k, vt, bias_tab, tok0, n_seq, seq_len):
    rows = seq_len // GRID_W
    pb = ATT_PAIRS_PER_STEP
    assert rows >= WIN_ROWS and rows % (2 * pb) == 0 and tok0 % seq_len == 0 and 2 * GRID_W == LANES
    steps = rows // (2 * pb)
    s0 = tok0 // seq_len
    qb0 = tok0 // (pb * LANES)
    return pl.pallas_call(
        functools.partial(_att_kernel, rows=rows, pb=pb),
        grid=(n_seq, steps),
        in_specs=[pl.BlockSpec((ATT_W, pb * LANES), lambda b, j: (0, qb0 + b * steps + j)),
                  pl.BlockSpec((seq_len, ATT_W), lambda b, j: (s0 + b, 0)),
                  pl.BlockSpec((ATT_W, seq_len), lambda b, j: (0, s0 + b)),
                  pl.BlockSpec(bias_tab.shape, lambda b, j: (0, 0, 0, 0))],
        out_specs=pl.BlockSpec((pb * LANES, ATT_W), lambda b, j: (b * steps + j, 0)),
        out_shape=jax.ShapeDtypeStruct((n_seq * seq_len, ATT_W), BF16),
        compiler_params=_cparams(("parallel", "parallel")), name="nbr_attention",
    )(qt, k, vt, bias_tab)


def _layer_norm(x, g, b):
    mu = jnp.mean(x, axis=-1, keepdims=True)
    xc = x - mu
    var = jnp.mean(xc * xc, axis=-1, keepdims=True)
    return xc * lax.rsqrt(var + LN_EPS) * g + b


def _outproj_kernel(*refs, alpha, bounds):
    n_src = len(bounds)
    (pool_ref, yf_ref, yb_ref, xs_ref, z_ref, att_ref, wo_ref, dskip_ref, ng_ref, g_ref, b_ref, wr_ref, br_ref,
     h1_ref, h1b_ref, idx_ref, rank_ref, gate_ref, cnt_ref) = refs[n_src:]
    h_in = _pick_source(refs[:n_src], bounds)

    @pl.when(pl.program_id(0) == 0)
    def _():
        cnt_ref[...] = jnp.zeros(cnt_ref.shape, F32)

    y = yf_ref[...] + yb_ref[...] + xs_ref[...] * dskip_ref[...]
    z = z_ref[...]
    y = y * (z * jax.nn.sigmoid(z))
    parts = []
    for g in range(SSD_GROUPS):
        yg = y[:, g * SSD_GW:(g + 1) * SSD_GW]
        parts.append(yg * lax.rsqrt(jnp.mean(yg * yg, axis=-1, keepdims=True) + RMS_EPS))
    ssd = (jnp.concatenate(parts, axis=1) * ng_ref[...]).astype(BF16)
    m = (_dot(pool_ref[...].astype(BF16), wo_ref[0:POOL_W, :])
         + _dot(ssd, wo_ref[POOL_W:POOL_W + SSD_INNER, :])
         + _dot(att_ref[...], wo_ref[POOL_W + SSD_INNER:, :]))
    h1 = _layer_norm(alpha * h_in + m, g_ref[...], b_ref[...])
    h1_ref[...] = h1
    h1_hi = h1.astype(BF16)
    h1b_ref[...] = h1_hi
    h1_lo = (h1 - h1_hi.astype(F32)).astype(BF16)

    hh = _dot_nt(wr_ref[...], h1_hi)
    logits = hh[:N_EXPERTS] + hh[N_EXPERTS:] + _dot_nt(wr_ref[0:N_EXPERTS, :], h1_lo) + br_ref[...]
    eidx = lax.broadcasted_iota(jnp.int32, logits.shape, 0)
    vals, idxs = [], []
    cur = logits
    for _ in range(TOP_K):
        mx = jnp.max(cur, axis=0, keepdims=True)
        ix = jnp.min(jnp.where(cur == mx, eidx, N_EXPERTS), axis=0, keepdims=True)
        vals.append(mx)
        idxs.append(ix)
        cur = jnp.where(eidx == ix, -jnp.inf, cur)
    es = [jnp.exp(vv - vals[0]) for vv in vals]
    den = es[0] + es[1] + es[2] + es[3]
    tm = logits.shape[1]
    idx_ref[...] = jnp.concatenate(idxs + [jnp.zeros((8 - TOP_K, tm), jnp.int32)], axis=0)

    ti = lax.broadcasted_iota(jnp.int32, (LANES, LANES), 0)
    tj = lax.broadcasted_iota(jnp.int32, (LANES, LANES), 1)
    before = (ti < tj).astype(BF16)
    onehots = [(eidx == ix).astype(F32) for ix in idxs]
    sel = onehots[0] + onehots[1] + onehots[2] + onehots[3]
    base = cnt_ref[...]
    ranks = [[] for _ in range(TOP_K)]
    for q in range(tm // LANES):
        ls = slice(q * LANES, (q + 1) * LANES)
        prefix = _dot(sel[:, ls].astype(BF16), before) + base
        for k in range(TOP_K):
            ranks[k].append(jnp.sum(onehots[k][:, ls] * prefix, axis=0, keepdims=True))
        base = base + jnp.sum(sel[:, ls], axis=1, keepdims=True)
    cnt_ref[...] = base
    rank_rows = [jnp.concatenate(r, axis=1) for r in ranks] + [jnp.zeros((8 - TOP_K, tm), F32)]
    rank_ref[...] = jnp.concatenate(rank_rows, axis=0).astype(jnp.int32)

    gates_t = jnp.concatenate([e / den for e in es] + [jnp.zeros((LANES - TOP_K, tm), F32)], axis=0)
    gate_ref[...] = gates_t.T


def _out_proj_router(srcs, pool_out, y_f, y_b, xs, z, att, w_out, dskip, norm_g, ln_g, ln_b, w_router_t, b_router, alpha):
    t = pool_out.shape[0]
    tm = TM_OUT
    row = lambda w: pl.BlockSpec((tm, w), lambda i: (i, 0))
    full = lambda a: pl.BlockSpec(a.shape, lambda i: (0,) * a.ndim)
    consts = (w_out, dskip, norm_g, ln_g, ln_b, w_router_t, b_router)
    src_specs, bounds = _token_tile_specs(srcs, tm)
    return pl.pallas_call(
        functools.partial(_outproj_kernel, alpha=alpha, bounds=bounds),
        grid=(t // tm,),
        in_specs=src_specs + [row(POOL_W), row(SSD_INNER), row(SSD_INNER), row(SSD_INNER), row(SSD_INNER),
                              row(ATT_W)] + [full(a) for a in consts],
        out_specs=[row(D_MODEL), row(D_MODEL), pl.BlockSpec((8, tm), lambda i: (0, i)),
                   pl.BlockSpec((8, tm), lambda i: (0, i)), row(LANES),
                   pl.BlockSpec((N_EXPERTS, LANES), lambda i: (0, 0))],
        out_shape=[jax.ShapeDtypeStruct((t, D_MODEL), F32), jax.ShapeDtypeStruct((t, D_MODEL), BF16),
                   jax.ShapeDtypeStruct((8, t), jnp.int32), jax.ShapeDtypeStruct((8, t), jnp.int32),
                   jax.ShapeDtypeStruct((t, LANES), F32), jax.ShapeDtypeStruct((N_EXPERTS, LANES), F32)],
        compiler_params=_cparams(("arbitrary",)), name="out_proj_router",
    )(*srcs, pool_out, y_f, y_b, xs, z, att, *consts)


def _expert_kernel(blk_ref, e_ref, lo_ref, hi_ref, first_ref, newe_ref, x_ref, w1_ref, b1g_ref, b1l_ref,
                   w2_ref, b2_ref, perm_ref, o_ref, w1g_s, w1l_s, w2_s):
    w = pl.program_id(0)
    lo = lo_ref[w]
    hi = hi_ref[w]

    @pl.when(newe_ref[w] == 1)
    def _():
        for j in range(D_FF // LANES):
            chunk = w1_ref[:, 2 * LANES * j:2 * LANES * (j + 1)].astype(BF16)
            r = _dot(chunk, perm_ref[...])
            w1g_s[:, LANES * j:LANES * (j + 1)] = r[:, :LANES].astype(BF16)
            w1l_s[:, LANES * j:LANES * (j + 1)] = r[:, LANES:].astype(BF16)
        w2_s[...] = w2_ref[...].astype(BF16)

    @pl.when(hi > lo)
    def _():
        x = x_ref[...]
        acc = jnp.zeros(o_ref.shape, F32) + b2_ref[...]
        for c in range(0, D_FF, FF_CHUNK):
            hg = _dot(x, w1g_s[:, c:c + FF_CHUNK]) + b1g_ref[:, c:c + FF_CHUNK]
            hl = _dot(x, w1l_s[:, c:c + FF_CHUNK]) + b1l_ref[:, c:c + FF_CHUNK]
            hg = jnp.minimum(hg, SWIGLU_LIMIT)
            hl = jnp.clip(hl, -SWIGLU_LIMIT, SWIGLU_LIMIT)
            act = hg * jax.nn.sigmoid(SWIGLU_ALPHA * hg) * (hl + 1.0)
            acc = acc + _dot(act.astype(BF16), w2_s[c:c + FF_CHUNK, :])
        rows = blk_ref[w] * MOE_BM + lax.broadcasted_iota(jnp.int32, (MOE_BM, 1), 0)
        mine = (rows >= lo) & (rows < hi)
        y = acc.astype(o_ref.dtype)

        @pl.when(first_ref[w] == 1)
        def _():
            o_ref[...] = jnp.where(mine, y, jnp.zeros_like(y))

        @pl.when(first_ref[w] == 0)
        def _():
            o_ref[...] = jnp.where(mine, y, o_ref[...])


def _experts(x_sorted, meta, layer, w1, b1g, b1l, w2, b2):
    n_rows = x_sorted.shape[0]
    bm = MOE_BM
    n_items = meta[0].shape[0]
    o = np.arange(2 * LANES)[None, :]
    c = np.arange(2 * LANES)[:, None]
    perm = jnp.asarray(np.where(o < LANES, c == 2 * o, c == 2 * (o - LANES) + 1), BF16)
    wspec = lambda r, c: pl.BlockSpec((None, None, r, c), lambda w, blk, e, *_: (layer, e[w], 0, 0))
    bspec = lambda c: pl.BlockSpec((None, 1, c), lambda w, blk, e, *_: (e[w], 0, 0))
    grid_spec = pltpu.PrefetchScalarGridSpec(
        num_scalar_prefetch=6, grid=(n_items,),
        in_specs=[pl.BlockSpec((bm, D_MODEL), lambda w, blk, *_: (blk[w], 0)),
                  wspec(D_MODEL, 2 * D_FF), bspec(D_FF), bspec(D_FF), wspec(D_FF, D_MODEL), bspec(D_MODEL),
                  pl.BlockSpec(perm.shape, lambda w, *_: (0, 0))],
        out_specs=pl.BlockSpec((bm, D_MODEL), lambda w, blk, *_: (blk[w], 0)),
        scratch_shapes=[pltpu.VMEM((D_MODEL, D_FF), BF16), pltpu.VMEM((D_MODEL, D_FF), BF16),
                        pltpu.VMEM((D_FF, D_MODEL), BF16)])
    return pl.pallas_call(
        _expert_kernel, grid_spec=grid_spec,
        out_shape=jax.ShapeDtypeStruct((n_rows, D_MODEL), BF16),
        compiler_params=_cparams(("arbitrary",)), name="moe_experts",
    )(*meta, x_sorted, w1, b1g, b1l, w2, b2, perm)


def _combine_kernel(h_ref, y_ref, gate_ref, g_ref, b_ref, o_ref, *, alpha):
    gates = gate_ref[...]
    f = y_ref[0].astype(F32) * gates[:, 0:1]
    for k in range(1, TOP_K):
        f = f + y_ref[k].astype(F32) * gates[:, k:k + 1]
    o_ref[...] = _layer_norm(alpha * h_ref[...] + f, g_ref[...], b_ref[...])


def _combine_ln(h1, y_tok, gate_col, ln_g, ln_b, alpha, tok0, n_tok):
    tm = TM_OUT
    assert tok0 % tm == 0 and n_tok % tm == 0
    b0 = tok0 // tm
    return pl.pallas_call(
        functools.partial(_combine_kernel, alpha=alpha),
        grid=(n_tok // tm,),
        in_specs=[pl.BlockSpec((tm, D_MODEL), lambda i: (b0 + i, 0)),
                  pl.BlockSpec((TOP_K, tm, D_MODEL), lambda i: (0, b0 + i, 0)),
                  pl.BlockSpec((tm, LANES), lambda i: (b0 + i, 0)),
                  pl.BlockSpec((1, D_MODEL), lambda i: (0, 0)),
                  pl.BlockSpec((1, D_MODEL), lambda i: (0, 0))],
        out_specs=pl.BlockSpec((tm, D_MODEL), lambda i: (i, 0)),
        out_shape=jax.ShapeDtypeStruct((n_tok, D_MODEL), F32),
        compiler_params=_cparams(("parallel",)), name="combine_ln",
    )(h1, y_tok, gate_col, ln_g, ln_b)


def _routing(idx_t, rank_t, counts):
    t = idx_t.shape[1]
    n_asg = TOP_K * t
    bm = MOE_BM
    n_blocks = n_asg // bm
    n_items = n_blocks + N_EXPERTS - 1
    i32 = jnp.int32
    idx = idx_t[:TOP_K]
    tok = jnp.arange(t, dtype=i32)[None, :]
    skeys = lax.sort((idx * t + tok).reshape(-1))
    src_tok = skeys % t
    ends = jnp.cumsum(counts.astype(i32))
    starts = ends - counts.astype(i32)
    experts = jnp.arange(N_EXPERTS, dtype=i32)[:, None, None]
    pos = (jnp.sum(jnp.where(idx[None] == experts, starts[:, None, None], 0), axis=0) + rank_t[:TOP_K]).reshape(-1)
    first_tile = starts // bm
    ntiles = jnp.where(ends > starts, (ends + bm - 1) // bm - first_tile, 0)
    cum = jnp.cumsum(ntiles)
    base = cum - ntiles
    total = cum[-1]
    w = jnp.arange(n_items, dtype=i32)
    valid = w < total
    e_w = jnp.minimum(jnp.sum((cum[None, :] <= w[:, None]).astype(i32), axis=1), N_EXPERTS - 1)
    e_last = jnp.sum(jnp.where(w == total - 1, e_w, 0))
    e_w = jnp.where(valid, e_w, e_last)
    onehot_e = e_w[:, None] == jnp.arange(N_EXPERTS, dtype=i32)[None, :]
    at_e = lambda table: jnp.sum(jnp.where(onehot_e, table[None, :], 0), axis=1)
    blk = jnp.where(valid, at_e(first_tile) + (w - at_e(base)), n_blocks - 1).astype(i32)
    lo = jnp.where(valid, jnp.maximum(at_e(starts), blk * bm), 0).astype(i32)
    hi = jnp.where(valid, jnp.minimum(at_e(ends), (blk + 1) * bm), 0).astype(i32)
    one = jnp.ones((1,), i32)
    first = jnp.concatenate([one, (blk[1:] != blk[:-1]).astype(i32)])
    new_e = jnp.concatenate([one, (e_w[1:] != e_w[:-1]).astype(i32)])
    return src_tok, pos, (blk, e_w, lo, hi, first, new_e)


def _prep_layer(i, w_in, conv_w, conv_b, a_log, dt_bias, w_pool, pool_scale, rpb, w_out, w_router, b_router,
                w1, b1, w2, b2, d_skip, ssd_norm_g, ln1_g, ln1_b, ln2_g, ln2_b):
    c_x = POOL_W + SSD_INNER + SSD_XBC
    c_dt = c_x + DT_COLS
    w = w_in[i]
    w_dt = w[:, c_x:c_dt]
    w_q, w_k, w_v = (w[:, c_dt + j * ATT_W:c_dt + (j + 1) * ATT_W] for j in range(3))
    pad = LANES - DT_REP * DT_COLS
    w_all = jnp.concatenate([w[:, :c_x], w_k] + [w_dt] * DT_REP + [jnp.zeros((D_MODEL, pad), F32)],
                            axis=1).astype(BF16)
    w_t = jnp.concatenate([w_q, w_v, w_dt], axis=1).T.astype(BF16)
    packed_row = lambda v: jnp.concatenate([v.reshape(1, DT_COLS)] * DT_REP + [jnp.zeros((1, pad), F32)], axis=1)
    w_bd = jnp.zeros((POOL_W, POOL_W), F32)
    for g in range(POOL_GROUPS):
        sl = slice(g * POOL_GDIM, (g + 1) * POOL_GDIM)
        w_bd = w_bd.at[sl, sl].set(w_pool[i, g])
    conv_w_pad = jnp.concatenate([conv_w[i], jnp.zeros((HALO - SSD_CONV, SSD_XBC), F32)], axis=0)
    return dict(
        layer=i, w1=w1, w2=w2,
        w_all=w_all, w_t=w_t, w_bd=w_bd.astype(BF16), pool_scale=pool_scale[i].reshape(1, POOL_W),
        conv_w=conv_w_pad, conv_b=conv_b[i].reshape(1, SSD_XBC),
        dt_bias_row=packed_row(dt_bias[i]), dt_bias_col=dt_bias[i].reshape(DT_COLS, 1),
        alog_row=packed_row(a_log[i]), alog_col=a_log[i].reshape(DT_COLS, 1),
        bias_tab=_att_bias_table(rpb[i]),
        w_out=w_out[i].astype(BF16), dskip=jnp.repeat(d_skip[i], SSD_HDIM).reshape(1, SSD_INNER),
        norm_g=ssd_norm_g[i].reshape(1, SSD_INNER), ln1_g=ln1_g[i].reshape(1, D_MODEL), ln1_b=ln1_b[i].reshape(1, D_MODEL),
        w_router_t=jnp.concatenate(_split3(w_router[i].T)[:2], axis=0), b_router=b_router[i].reshape(N_EXPERTS, 1),
        b1g=b1[i, :, 0::2].reshape(N_EXPERTS, 1, D_FF), b1l=b1[i, :, 1::2].reshape(N_EXPERTS, 1, D_FF),
        b2=b2[i].reshape(N_EXPERTS, 1, D_MODEL),
        ln2_g=ln2_g[i].reshape(1, D_MODEL), ln2_b=ln2_b[i].reshape(1, D_MODEL))


def _encoder_layer(srcs, p, seq_groups, alpha, split_out):
    t = sum(a.shape[0] for a in srcs)
    seq_lens = tuple(s for tok0, n, s in seq_groups for _ in range(n))
    u, z, xbc, k, dt, qt, vt, dtt = _in_proj(srcs, p['w_all'], p['w_t'])
    pool_out = _pool_mixer(u, p['w_bd'], p['pool_scale'], seq_lens)
    xs, xcf, xcb, c, bt = _conv_silu(xbc, dt, p['conv_w'], p['conv_b'], p['dt_bias_row'], seq_lens)
    y_f, y_b = _ssd_scan(xcf, xcb, c, bt, dt, dtt, p['dt_bias_row'], p['dt_bias_col'], p['alog_row'], p['alog_col'],
                         seq_lens)
    att = jnp.concatenate([_attention(qt, k, vt, p['bias_tab'], tok0, n, s) for tok0, n, s in seq_groups], axis=0)
    h1, h1b, idx_t, rank_t, gate_col, cnt = _out_proj_router(
        srcs, pool_out, y_f, y_b, xs, z, att, p['w_out'], p['dskip'], p['norm_g'], p['ln1_g'], p['ln1_b'],
        p['w_router_t'], p['b_router'], alpha)
    src_tok, pos, meta = _routing(idx_t, rank_t, cnt[:, 0])
    x_sorted = jnp.take(h1b, src_tok, axis=0, mode='clip')
    y_sorted = _experts(x_sorted, meta, p['layer'], p['w1'], p['b1g'], p['b1l'], p['w2'], p['b2'])
    y_tok = jnp.take(y_sorted, pos, axis=0, mode='clip').reshape(TOP_K, t, D_MODEL)
    ranges = [(tok0, n * s) for tok0, n, s in seq_groups] if split_out else [(0, t)]
    return [_combine_ln(h1, y_tok, gate_col, p['ln2_g'], p['ln2_b'], alpha, tok0, n_tok) for tok0, n_tok in ranges]


def kernel(x_prompt, x_sample, w_in, conv_w, conv_b, a_log, dt_bias, d_skip, ssd_norm_g, w_pool, pool_scale, rpb, w_out,
           ln1_g, ln1_b, w_router, b_router, w1, b1, w2, b2, ln2_g, ln2_b):
    depth = w_in.shape[0]
    alpha = (2 * depth) ** 0.25
    bp, sp, _ = x_prompt.shape
    bs, ss, _ = x_sample.shape
    seq_groups = ((0, bp, sp), (bp * sp, bs, ss))
    outs = [x_prompt.reshape(bp * sp, D_MODEL), x_sample.reshape(bs * ss, D_MODEL)]
    for i in range(depth):
        p = _prep_layer(i, w_in, conv_w, conv_b, a_log, dt_bias, w_pool, pool_scale, rpb, w_out, w_router, b_router,
                        w1, b1, w2, b2, d_skip, ssd_norm_g, ln1_g, ln1_b, ln2_g, ln2_b)
        outs = _encoder_layer(outs, p, seq_groups, alpha, split_out=(i == depth - 1))
    return (outs[0].reshape(bp, sp, D_MODEL), outs[1].reshape(bs, ss, D_MODEL))
```

```python
import functools

import numpy as np
import jax
import jax.numpy as jnp
from jax import lax
from jax.experimental import pallas as pl
from jax.experimental.pallas import tpu as pltpu

F32 = jnp.float32
BF16 = jnp.bfloat16

D_MODEL = 1024
POOL_GROUPS = 4
POOL_GDIM = 64
POOL_W = POOL_GROUPS * POOL_GDIM
POOL_WINDOWS = (2, 4, 8, 16)
SSD_HEADS = 8
SSD_HDIM = 64
SSD_INNER = SSD_HEADS * SSD_HDIM
SSD_GROUPS = 2
SSD_STATE = 128
SSD_CONV = 5
SSD_CHUNK = 128
SSD_BC = 2 * SSD_GROUPS * SSD_STATE
SSD_XBC = SSD_INNER + SSD_BC
SSD_GW = SSD_INNER // SSD_GROUPS
ATT_HEADS = 4
ATT_HDIM = 64
ATT_W = ATT_HEADS * ATT_HDIM
GRID_W = 64
NA_ROWS = 8
NA_COLS = 16
N_EXPERTS = 32
TOP_K = 4
D_FF = D_MODEL
SWIGLU_LIMIT = 7.0
SWIGLU_ALPHA = 1.702
LN_EPS = 1e-5
RMS_EPS = 1e-5

LANES = 128
DT_COLS = 2 * SSD_HEADS
DT_REP = 3
HALO = 8
NEG_BIG = -1e30
WIN_ROWS = NA_ROWS + 2
WIN_KEYS = WIN_ROWS * GRID_W

TM_PROJ = 512
TM_OUT = 512
SSD_CHUNKS_PER_STEP = 4
TS_SEQ = 512
ATT_PAIRS_PER_STEP = 4
MOE_BM = 512
FF_CHUNK = 1024
VMEM_LIMIT = 56 * 1024 * 1024


def _cparams(sem):
    return pltpu.CompilerParams(dimension_semantics=sem, vmem_limit_bytes=VMEM_LIMIT)


def _dot(a, b):
    return jnp.dot(a, b, preferred_element_type=F32)


def _dot_nt(a, b, precision=None):
    return lax.dot_general(a, b, (((1,), (1,)), ((), ())), preferred_element_type=F32, precision=precision)


def _softplus(v):
    return jnp.maximum(v, 0.0) + jnp.log1p(jnp.exp(-jnp.abs(v)))


def _split3(v):
    hi = v.astype(BF16)
    r1 = v - hi.astype(F32)
    mid = r1.astype(BF16)
    lo = (r1 - mid.astype(F32)).astype(BF16)
    return hi, mid, lo


def _pack3(v):
    hi, mid, lo = _split3(v)
    lane = lax.broadcasted_iota(jnp.int32, v.shape, 1)
    zero = jnp.zeros(v.shape, BF16)
    return jnp.where(lane < DT_COLS, hi, jnp.where(lane < 2 * DT_COLS, mid, jnp.where(lane < 3 * DT_COLS, lo, zero)))


def _expand_matrix(off, width):
    r = np.arange(LANES)[:, None]
    c = np.arange(SSD_HEADS * width)[None, :]
    m = (r < DT_REP * DT_COLS) & ((r % DT_COLS) == off + c // width)
    return jnp.asarray(m, BF16)


def _token_tile_specs(srcs, tm):
    specs, bounds, first = [], [], 0
    for a in srcs:
        n = a.shape[0] // tm
        assert a.shape[0] % tm == 0
        specs.append(pl.BlockSpec((tm, a.shape[1]), lambda i, first=first, n=n: (jnp.clip(i - first, 0, n - 1), 0)))
        first += n
        bounds.append(first)
    return specs, tuple(bounds)


def _pick_source(refs, bounds):
    x = refs[-1][...]
    for ref, end in zip(reversed(refs[:-1]), reversed(bounds[:-1])):
        x = jnp.where(pl.program_id(0) < end, ref[...], x)
    return x


def _inproj_kernel(*refs, bounds):
    n_src = len(bounds)
    w_ref, wt_ref, u_ref, z_ref, xbc_ref, k_ref, dt_ref, qt_ref, vt_ref, dtt_ref = refs[n_src:]
    xb = _pick_source(refs[:n_src], bounds).astype(BF16)
    c0 = 0
    for ref, width in ((u_ref, POOL_W), (z_ref, SSD_INNER), (xbc_ref, SSD_XBC), (k_ref, ATT_W), (dt_ref, LANES)):
        ref[...] = _dot(xb, w_ref[:, c0:c0 + width]).astype(ref.dtype)
        c0 += width
    tr = _dot_nt(wt_ref[...], xb)
    qt_ref[...] = (tr[:ATT_W] * ATT_HDIM ** -0.5).astype(BF16)
    vt_ref[...] = tr[ATT_W:2 * ATT_W].astype(BF16)
    dtt_ref[...] = tr[2 * ATT_W:]


def _in_proj(srcs, w_all, w_t):
    t = sum(a.shape[0] for a in srcs)
    tm = TM_PROJ
    row = lambda w: pl.BlockSpec((tm, w), lambda i: (i, 0))
    col = lambda r: pl.BlockSpec((r, tm), lambda i: (0, i))
    full = lambda a: pl.BlockSpec(a.shape, lambda i: (0,) * a.ndim)
    src_specs, bounds = _token_tile_specs(srcs, tm)
    return pl.pallas_call(
        functools.partial(_inproj_kernel, bounds=bounds),
        grid=(t // tm,),
        in_specs=src_specs + [full(w_all), full(w_t)],
        out_specs=[row(POOL_W), row(SSD_INNER), row(SSD_XBC), row(ATT_W), row(LANES),
                   col(ATT_W), col(ATT_W), col(DT_COLS)],
        out_shape=[jax.ShapeDtypeStruct((t, POOL_W), F32), jax.ShapeDtypeStruct((t, SSD_INNER), F32),
                   jax.ShapeDtypeStruct((t, SSD_XBC), F32), jax.ShapeDtypeStruct((t, ATT_W), BF16),
                   jax.ShapeDtypeStruct((t, LANES), F32), jax.ShapeDtypeStruct((ATT_W, t), BF16),
                   jax.ShapeDtypeStruct((ATT_W, t), BF16), jax.ShapeDtypeStruct((DT_COLS, t), F32)],
        compiler_params=_cparams(("parallel",)),
        name="in_proj",
    )(*srcs, w_all, w_t)


def _seq_tables(seq_lens, tile):
    pos, length = [], []
    for s in seq_lens:
        assert s % tile == 0
        for p in range(0, s, tile):
            pos.append(p)
            length.append(s)
    return np.asarray(pos, np.int32), np.asarray(length, np.int32)


def _halo_specs(ts, width, n_tok):
    hb = ts // HALO
    last = n_tok // HALO - 1
    cur = pl.BlockSpec((ts, width), lambda i, *_: (i, 0))
    prev = pl.BlockSpec((HALO, width), lambda i, *_: (jnp.maximum(i * hb - 1, 0), 0))
    nxt = pl.BlockSpec((HALO, width), lambda i, *_: (jnp.minimum((i + 1) * hb, last), 0))
    return prev, cur, nxt


def _extended(prev_ref, cur_ref, next_ref, pos0, slen, ts):
    prev = jnp.where(pos0 > 0, prev_ref[...], 0.0)
    nxt = jnp.where(pos0 + ts < slen, next_ref[...], 0.0)
    return jnp.concatenate([prev, cur_ref[...], nxt], axis=0)


def _rows_from(ext, k, ts):
    n = ext.shape[0]
    shift = (n - (HALO + k)) % n
    rolled = ext if shift == 0 else pltpu.roll(ext, shift, 0)
    return rolled[:ts]


def _pool_kernel(pos_ref, len_ref, prev_ref, cur_ref, next_ref, w_ref, scale_ref, o_ref, *, ts):
    i = pl.program_id(0)
    pos0 = pos_ref[i]
    slen = len_ref[i]
    ext = _extended(prev_ref, cur_ref, next_ref, pos0, slen, ts)
    n = ext.shape[0]

    def pair_sum(a, k):
        return a + pltpu.roll(a, n - k, 0)

    a2 = pair_sum(ext, 1)
    a4 = pair_sum(a2, 2)
    a8 = pair_sum(a4, 4)
    a16 = pair_sum(a8, 8)
    sums = [_rows_from(a, -w // 2, ts) for a, w in zip((a2, a4, a8, a16), POOL_WINDOWS)]

    lane = lax.broadcasted_iota(jnp.int32, (ts, POOL_W), 1)
    grp = lane // POOL_GDIM
    wsum = jnp.where(grp == 0, sums[0], jnp.where(grp == 1, sums[1], jnp.where(grp == 2, sums[2], sums[3])))
    half = jnp.where(grp == 0, 1, jnp.where(grp == 1, 2, jnp.where(grp == 2, 4, 8)))
    tpos = pos0 + lax.broadcasted_iota(jnp.int32, (ts, POOL_W), 0)
    cnt = (jnp.minimum(tpos + half, slen) - jnp.maximum(tpos - half, 0)).astype(F32)
    pooled = wsum / cnt - cur_ref[...]
    o_ref[...] = _dot(pooled.astype(BF16), w_ref[...]) * scale_ref[...]


def _pool_mixer(u, w_bd, scale, seq_lens):
    t = u.shape[0]
    ts = TS_SEQ
    pos, length = _seq_tables(seq_lens, ts)
    prev, cur, nxt = _halo_specs(ts, POOL_W, t)
    grid_spec = pltpu.PrefetchScalarGridSpec(
        num_scalar_prefetch=2, grid=(t // ts,),
        in_specs=[prev, cur, nxt,
                  pl.BlockSpec((POOL_W, POOL_W), lambda i, *_: (0, 0)),
                  pl.BlockSpec((1, POOL_W), lambda i, *_: (0, 0))],
        out_specs=pl.BlockSpec((ts, POOL_W), lambda i, *_: (i, 0)))
    return pl.pallas_call(
        functools.partial(_pool_kernel, ts=ts), grid_spec=grid_spec,
        out_shape=jax.ShapeDtypeStruct((t, POOL_W), F32),
        compiler_params=_cparams(("parallel",)), name="pool_mixer",
    )(pos, length, u, u, u, w_bd, scale)


def _conv_kernel(pos_ref, len_ref, prev_ref, cur_ref, next_ref, dt_ref, w_ref, b_ref, dtb_ref, ef_ref, eb_ref,
                 xs_ref, xcf_ref, xcb_ref, c_ref, bt_ref, *, ts):
    i = pl.program_id(0)
    ext = _extended(prev_ref, cur_ref, next_ref, pos_ref[i], len_ref[i], ts)
    acc = jnp.zeros((ts, SSD_XBC), F32) + b_ref[...]
    for j in range(SSD_CONV):
        acc = acc + _rows_from(ext, j - SSD_CONV // 2, ts) * w_ref[j:j + 1, :]
    act = acc * jax.nn.sigmoid(acc)
    xs = act[:, :SSD_INNER]
    xs_ref[...] = xs
    dt_pk = _pack3(_softplus(dt_ref[...] + dtb_ref[...]))
    xcf_ref[...] = (xs * _dot(dt_pk, ef_ref[...])).astype(BF16)
    xcb_ref[...] = (xs * _dot(dt_pk, eb_ref[...])).astype(BF16)
    nb = SSD_GROUPS * SSD_STATE
    bt_ref[...] = act[:, SSD_INNER:SSD_INNER + nb].T.astype(BF16)
    c_ref[...] = act[:, SSD_INNER + nb:].astype(BF16)


def _conv_silu(xbc, dt, conv_w, conv_b, dt_bias_row, seq_lens):
    t = xbc.shape[0]
    ts = TS_SEQ
    pos, length = _seq_tables(seq_lens, ts)
    prev, cur, nxt = _halo_specs(ts, SSD_XBC, t)
    e_f = _expand_matrix(0, SSD_HDIM)
    e_b = _expand_matrix(SSD_HEADS, SSD_HDIM)
    nb = SSD_GROUPS * SSD_STATE
    const = lambda a: pl.BlockSpec(a.shape, lambda i, *_: (0, 0))
    row = lambda w: pl.BlockSpec((ts, w), lambda i, *_: (i, 0))
    grid_spec = pltpu.PrefetchScalarGridSpec(
        num_scalar_prefetch=2, grid=(t // ts,),
        in_specs=[prev, cur, nxt, row(LANES), const(conv_w), const(conv_b), const(dt_bias_row), const(e_f), const(e_b)],
        out_specs=[row(SSD_INNER), row(SSD_INNER), row(SSD_INNER), row(nb),
                   pl.BlockSpec((nb, ts), lambda i, *_: (0, i))])
    return pl.pallas_call(
        functools.partial(_conv_kernel, ts=ts), grid_spec=grid_spec,
        out_shape=[jax.ShapeDtypeStruct((t, SSD_INNER), F32), jax.ShapeDtypeStruct((t, SSD_INNER), BF16),
                   jax.ShapeDtypeStruct((t, SSD_INNER), BF16), jax.ShapeDtypeStruct((t, nb), BF16),
                   jax.ShapeDtypeStruct((nb, t), BF16)],
        compiler_params=_cparams(("parallel",)), name="conv_silu",
    )(pos, length, xbc, xbc, xbc, dt, conv_w, conv_b, dt_bias_row, e_f, e_b)


def _ssd_chunk(xc, c, bt, dt_full, dtt, aneg_full, aneg_col, e128, state_ref, reset, reverse, off):
    L = SSD_CHUNK
    ii = lax.broadcasted_iota(jnp.int32, (L, L), 0)
    jj = lax.broadcasted_iota(jnp.int32, (L, L), 1)
    tri = (jj >= ii) if reverse else (jj <= ii)
    tri_t = (ii >= jj) if reverse else (ii <= jj)
    tri_b = tri.astype(BF16)
    tri_tb = tri_t.astype(BF16)
    acs = sum(_dot(tri_b, part) for part in _split3(dt_full * aneg_full))
    acs_row = sum(_dot(part, tri_tb) for part in _split3(dtt * aneg_col))
    edge = 0 if reverse else L - 1
    acs_b = _dot(_pack3(acs), e128)
    lane = lax.broadcasted_iota(jnp.int32, (L, LANES), 1)
    low = lane < SSD_HDIM
    acs_e = jnp.concatenate(
        [jnp.where(low, acs_b[:, (2 * j) * LANES:(2 * j + 1) * LANES], acs_b[:, (2 * j + 1) * LANES:(2 * j + 2) * LANES])
         for j in range(SSD_HEADS // 2)], axis=1)
    tot_e = acs_e[edge:edge + 1, :]
    exp_acs = jnp.exp(acs_e)
    to_end = jnp.exp(tot_e - acs_e)
    chunk_decay = jnp.exp(tot_e)
    xw = (xc.astype(F32) * to_end).astype(BF16)
    zero_b = jnp.zeros((L, LANES), BF16)

    ys = []
    for g in range(SSD_GROUPS):
        gs = slice(g * SSD_GW, (g + 1) * SSD_GW)
        c_g = c[:, g * SSD_STATE:(g + 1) * SSD_STATE]
        bt_g = bt[g * SSD_STATE:(g + 1) * SSD_STATE, :]
        cb = _dot(c_g, bt_g)
        prev = state_ref[g] if reset is None else jnp.where(reset, 0.0, state_ref[g])
        y_off = _dot(c_g, prev.astype(BF16)) * exp_acs[:, gs]
        state_ref[g] = prev * chunk_decay[:, gs] + _dot(bt_g, xw[:, gs])
        for pair in range(SSD_GW // LANES):
            t0 = g * SSD_GW + pair * LANES
            tile = xc[:, t0:t0 + LANES]
            halves = (jnp.where(low, tile, zero_b), jnp.where(low, zero_b, tile))
            y_pair = y_off[:, pair * LANES:(pair + 1) * LANES]
            for s in range(2):
                h = t0 // SSD_HDIM + s
                seg = acs_b[:, h * LANES:(h + 1) * LANES] - acs_row[off + h:off + h + 1, :]
                m = (cb * jnp.exp(jnp.where(tri, seg, NEG_BIG))).astype(BF16)
                y_pair = y_pair + _dot(m, halves[s])
            ys.append(y_pair)
    return jnp.concatenate(ys, axis=1)


def _ssd_kernel(mir_ref, start_ref, xc_f, c_f, bt_f, dt_f, dtt_f, xc_b, c_b, bt_b, dt_b, dtt_b,
                bias_row, bias_col, alog_row, alog_col, ef_ref, eb_ref, yf_ref, yb_ref, st_f, st_b):
    step = pl.program_id(0)
    L = SSD_CHUNK
    first = start_ref[step] == 1
    aneg_full = -jnp.exp(alog_row[...])
    aneg_col = -jnp.exp(alog_col[...])
    dirs = ((xc_f, c_f, bt_f, dt_f, dtt_f, ef_ref, yf_ref, st_f, False, 0),
            (xc_b, c_b, bt_b, dt_b, dtt_b, eb_ref, yb_ref, st_b, True, SSD_HEADS))
    for n in range(SSD_CHUNKS_PER_STEP):
        for xc, c, bt, dt, dtt, e_ref, y_ref, st, reverse, off in dirs:
            sub = SSD_CHUNKS_PER_STEP - 1 - n if reverse else n
            rows = slice(sub * L, (sub + 1) * L)
            dtc = _softplus(dt[rows, :] + bias_row[...])
            dtr = _softplus(dtt[:, rows] + bias_col[...])
            reset = first if n == 0 else None
            y_ref[rows, :] = _ssd_chunk(xc[rows, :], c[rows, :], bt[:, rows], dtc, dtr, aneg_full, aneg_col, e_ref[...],
                                        st, reset, reverse, off)


def _ssd_tables(seq_lens, block):
    mirror, start = [], []
    c0 = 0
    for s in seq_lens:
        assert s % block == 0
        n = s // block
        for c in range(n):
            mirror.append(c0 + n - 1 - c)
            start.append(1 if c == 0 else 0)
        c0 += n
    return np.asarray(mirror, np.int32), np.asarray(start, np.int32)


def _ssd_scan(xcf, xcb, c, bt, dt, dtt, dt_bias_row, dt_bias_col, alog_row, alog_col, seq_lens):
    t = xcf.shape[0]
    L = SSD_CHUNK * SSD_CHUNKS_PER_STEP
    nb = SSD_GROUPS * SSD_STATE
    mirror, start = _ssd_tables(seq_lens, L)
    e_f = _expand_matrix(0, LANES)
    e_b = _expand_matrix(SSD_HEADS, LANES)
    fw = lambda w: pl.BlockSpec((L, w), lambda s, m, st: (s, 0))
    bw = lambda w: pl.BlockSpec((L, w), lambda s, m, st: (m[s], 0))
    fw_t = lambda r: pl.BlockSpec((r, L), lambda s, m, st: (0, s))
    bw_t = lambda r: pl.BlockSpec((r, L), lambda s, m, st: (0, m[s]))
    const = lambda a: pl.BlockSpec(a.shape, lambda s, m, st: (0, 0))
    consts = (dt_bias_row, dt_bias_col, alog_row, alog_col, e_f, e_b)
    grid_spec = pltpu.PrefetchScalarGridSpec(
        num_scalar_prefetch=2, grid=(t // L,),
        in_specs=[fw(SSD_INNER), fw(nb), fw_t(nb), fw(LANES), fw_t(DT_COLS),
                  bw(SSD_INNER), bw(nb), bw_t(nb), bw(LANES), bw_t(DT_COLS)] + [const(a) for a in consts],
        out_specs=[fw(SSD_INNER), bw(SSD_INNER)],
        scratch_shapes=[pltpu.VMEM((SSD_GROUPS, SSD_STATE, SSD_GW), F32),
                        pltpu.VMEM((SSD_GROUPS, SSD_STATE, SSD_GW), F32)])
    return pl.pallas_call(
        _ssd_kernel, grid_spec=grid_spec,
        out_shape=[jax.ShapeDtypeStruct((t, SSD_INNER), F32), jax.ShapeDtypeStruct((t, SSD_INNER), F32)],
        compiler_params=_cparams(("arbitrary",)), name="ssd_scan",
    )(mirror, start, xcf, c, bt, dt, dtt, xcb, c, bt, dt, dtt, *consts)


ATT_VARIANTS = 5
_ATT_RS_OFF = ((0, -1), (-2, -3), (-4, -4), (-4, -5), (-6, -7))


def _att_bias_table(rpb):
    kc = np.arange(GRID_W)[:, None]
    c = np.arange(GRID_W)[None, :]
    cs = np.clip(c - NA_COLS // 2, 0, GRID_W - NA_COLS)
    valid = (kc >= cs) & (kc < cs + NA_COLS)
    d = kc - c + NA_COLS - 1
    onehot = (d[None] == np.arange(2 * NA_COLS - 1)[:, None, None]) & valid[None]
    tt = jnp.einsum('hrd,dkc->hrkc', rpb.astype(F32), jnp.asarray(onehot, F32),
                    precision=jax.lax.Precision.HIGHEST)
    tt = jnp.where(valid, tt, NEG_BIG)
    neg = jnp.full((ATT_HEADS, GRID_W, GRID_W), NEG_BIG, F32)
    variants = []
    for var in range(ATT_VARIANTS):
        halves = []
        for dq in range(2):
            tiles = []
            for i in range(WIN_ROWS):
                delta = i - 2 * var - dq
                inside = _ATT_RS_OFF[var][dq] <= delta < _ATT_RS_OFF[var][dq] + NA_ROWS
                tiles.append(tt[:, delta + NA_ROWS - 1] if inside else neg)
            halves.append(jnp.concatenate(tiles, axis=1))
        variants.append(jnp.concatenate(halves, axis=2))
    return jnp.stack(variants)


def _att_kernel(qt_ref, k_ref, vt_ref, bias_ref, o_ref, *, rows, pb):
    j = pl.program_id(1)
    sub = lax.broadcasted_iota(jnp.int32, (LANES, LANES), 0)
    low = sub < ATT_HDIM
    zero = jnp.zeros((LANES, LANES), BF16)
    edge = NA_ROWS // 2
    for pp in range(pb):
        r = (j * pb + pp) * 2
        ws = jnp.clip(r - edge, 0, rows - WIN_ROWS)
        var = jnp.where(r < edge, r // 2, jnp.where(r >= rows - edge, (r - (rows - edge)) // 2 + 3, 2))
        k0 = pl.multiple_of(ws * GRID_W, LANES)
        kwin = k_ref[pl.ds(k0, WIN_KEYS), :]
        vtw = vt_ref[:, pl.ds(k0, WIN_KEYS)]
        qt = qt_ref[:, pp * LANES:(pp + 1) * LANES]
        outs = []
        for jp in range(ATT_HEADS // 2):
            qpair = qt[jp * LANES:(jp + 1) * LANES, :]
            qboth = jnp.concatenate([jnp.where(low, qpair, zero), jnp.where(low, zero, qpair)], axis=1)
            s2 = _dot(kwin[:, jp * LANES:(jp + 1) * LANES], qboth)
            for hh in range(2):
                h = 2 * jp + hh
                s = s2[:, hh * LANES:(hh + 1) * LANES] + bias_ref[var, h]
                m = jnp.max(s, axis=0, keepdims=True)
                p = jnp.exp(s - m)
                den = jnp.sum(p, axis=0, keepdims=True)
                outs.append(_dot(vtw[h * ATT_HDIM:(h + 1) * ATT_HDIM, :], p.astype(BF16)) / den)
        ot = jnp.concatenate(outs, axis=0)
        o_ref[pp * LANES:(pp + 1) * LANES, :] = ot.T.astype(o_ref.dtype)


def _attention(qt, k, vt, bias_tab, tok0, n_seq, seq_len):
    rows = seq_len // GRID_W
    pb = ATT_PAIRS_PER_STEP
    assert rows >= WIN_ROWS and rows % (2 * pb) == 0 and tok0 % seq_len == 0 and 2 * GRID_W == LANES
    steps = rows // (2 * pb)
    s0 = tok0 // seq_len
    qb0 = tok0 // (pb * LANES)
    return pl.pallas_call(
        functools.partial(_att_kernel, rows=rows, pb=pb),
        grid=(n_seq, steps),
        in_specs=[pl.BlockSpec((ATT_W, pb * LANES), lambda b, j: (0, qb0 + b * steps + j)),
                  pl.BlockSpec((seq_len, ATT_W), lambda b, j: (s0 + b, 0)),
                  pl.BlockSpec((ATT_W, seq_len), lambda b, j: (0, s0 + b)),
                  pl.BlockSpec(bias_tab.shape, lambda b, j: (0, 0, 0, 0))],
        out_specs=pl.BlockSpec((pb * LANES, ATT_W), lambda b, j: (b * steps + j, 0)),
        out_shape=jax.ShapeDtypeStruct((n_seq * seq_len, ATT_W), BF16),
        compiler_params=_cparams(("parallel", "parallel")), name="nbr_attention",
    )(qt, k, vt, bias_tab)


def _layer_norm(x, g, b):
    mu = jnp.mean(x, axis=-1, keepdims=True)
    xc = x - mu
    var = jnp.mean(xc * xc, axis=-1, keepdims=True)
    return xc * lax.rsqrt(var + LN_EPS) * g + b


def _outproj_kernel(*refs, alpha, bounds, att_bounds):
    n_src, n_att = len(bounds), len(att_bounds)
    h_refs, att_refs = refs[:n_src], refs[n_src:n_src + n_att]
    (pool_ref, yf_ref, yb_ref, xs_ref, z_ref, wo_ref, dskip_ref, ng_ref, g_ref, b_ref, wr_ref, br_ref,
     h1_ref, h1b_ref, idx_ref, rank_ref, gate_ref, cnt_ref) = refs[n_src + n_att:]
    h_in = _pick_source(h_refs, bounds)
    att = _pick_source(att_refs, att_bounds)

    @pl.when(pl.program_id(0) == 0)
    def _():
        cnt_ref[...] = jnp.zeros(cnt_ref.shape, F32)

    y = yf_ref[...] + yb_ref[...] + xs_ref[...] * dskip_ref[...]
    z = z_ref[...]
    y = y * (z * jax.nn.sigmoid(z))
    parts = []
    for g in range(SSD_GROUPS):
        yg = y[:, g * SSD_GW:(g + 1) * SSD_GW]
        parts.append(yg * lax.rsqrt(jnp.mean(yg * yg, axis=-1, keepdims=True) + RMS_EPS))
    ssd = (jnp.concatenate(parts, axis=1) * ng_ref[...]).astype(BF16)
    m = (_dot(pool_ref[...].astype(BF16), wo_ref[0:POOL_W, :])
         + _dot(ssd, wo_ref[POOL_W:POOL_W + SSD_INNER, :])
         + _dot(att, wo_ref[POOL_W + SSD_INNER:, :]))
    h1 = _layer_norm(alpha * h_in + m, g_ref[...], b_ref[...])
    h1_ref[...] = h1
    h1_hi = h1.astype(BF16)
    h1b_ref[...] = h1_hi
    h1_lo = (h1 - h1_hi.astype(F32)).astype(BF16)

    hh = _dot_nt(wr_ref[...], h1_hi)
    logits = hh[:N_EXPERTS] + hh[N_EXPERTS:] + _dot_nt(wr_ref[0:N_EXPERTS, :], h1_lo) + br_ref[...]
    eidx = lax.broadcasted_iota(jnp.int32, logits.shape, 0)
    vals, idxs = [], []
    cur = logits
    for _ in range(TOP_K):
        mx = jnp.max(cur, axis=0, keepdims=True)
        ix = jnp.min(jnp.where(cur == mx, eidx, N_EXPERTS), axis=0, keepdims=True)
        vals.append(mx)
        idxs.append(ix)
        cur = jnp.where(eidx == ix, -jnp.inf, cur)
    es = [jnp.exp(vv - vals[0]) for vv in vals]
    den = es[0] + es[1] + es[2] + es[3]
    tm = logits.shape[1]
    idx_ref[...] = jnp.concatenate(idxs + [jnp.zeros((8 - TOP_K, tm), jnp.int32)], axis=0)

    ti = lax.broadcasted_iota(jnp.int32, (LANES, LANES), 0)
    tj = lax.broadcasted_iota(jnp.int32, (LANES, LANES), 1)
    before = (ti < tj).astype(BF16)
    onehots = [(eidx == ix).astype(F32) for ix in idxs]
    sel = onehots[0] + onehots[1] + onehots[2] + onehots[3]
    base = cnt_ref[...]
    ranks = [[] for _ in range(TOP_K)]
    for q in range(tm // LANES):
        ls = slice(q * LANES, (q + 1) * LANES)
        prefix = _dot(sel[:, ls].astype(BF16), before) + base
        for k in range(TOP_K):
            ranks[k].append(jnp.sum(onehots[k][:, ls] * prefix, axis=0, keepdims=True))
        base = base + jnp.sum(sel[:, ls], axis=1, keepdims=True)
    cnt_ref[...] = base
    rank_rows = [jnp.concatenate(r, axis=1) for r in ranks] + [jnp.zeros((8 - TOP_K, tm), F32)]
    rank_ref[...] = jnp.concatenate(rank_rows, axis=0).astype(jnp.int32)

    gates_t = jnp.concatenate([e / den for e in es] + [jnp.zeros((LANES - TOP_K, tm), F32)], axis=0)
    gate_ref[...] = gates_t.T


def _out_proj_router(srcs, pool_out, y_f, y_b, xs, z, atts, w_out, dskip, norm_g, ln_g, ln_b, w_router_t, b_router, alpha):
    t = pool_out.shape[0]
    tm = TM_OUT
    row = lambda w: pl.BlockSpec((tm, w), lambda i: (i, 0))
    full = lambda a: pl.BlockSpec(a.shape, lambda i: (0,) * a.ndim)
    consts = (w_out, dskip, norm_g, ln_g, ln_b, w_router_t, b_router)
    src_specs, bounds = _token_tile_specs(srcs, tm)
    att_specs, att_bounds = _token_tile_specs(atts, tm)
    return pl.pallas_call(
        functools.partial(_outproj_kernel, alpha=alpha, bounds=bounds, att_bounds=att_bounds),
        grid=(t // tm,),
        in_specs=src_specs + att_specs + [row(POOL_W), row(SSD_INNER), row(SSD_INNER), row(SSD_INNER),
                                          row(SSD_INNER)] + [full(a) for a in consts],
        out_specs=[row(D_MODEL), row(D_MODEL), pl.BlockSpec((8, tm), lambda i: (0, i)),
                   pl.BlockSpec((8, tm), lambda i: (0, i)), row(LANES),
                   pl.BlockSpec((N_EXPERTS, LANES), lambda i: (0, 0))],
        out_shape=[jax.ShapeDtypeStruct((t, D_MODEL), F32), jax.ShapeDtypeStruct((t, D_MODEL), BF16),
                   jax.ShapeDtypeStruct((8, t), jnp.int32), jax.ShapeDtypeStruct((8, t), jnp.int32),
                   jax.ShapeDtypeStruct((t, LANES), F32), jax.ShapeDtypeStruct((N_EXPERTS, LANES), F32)],
        compiler_params=_cparams(("arbitrary",)), name="out_proj_router",
    )(*srcs, *atts, pool_out, y_f, y_b, xs, z, *consts)


def _expert_kernel(blk_ref, e_ref, lo_ref, hi_ref, first_ref, newe_ref, x_ref, w1_ref, b1g_ref, b1l_ref,
                   w2_ref, b2_ref, perm_ref, o_ref, w1g_s, w1l_s, w2_s):
    w = pl.program_id(0)
    lo = lo_ref[w]
    hi = hi_ref[w]

    @pl.when(newe_ref[w] == 1)
    def _():
        for j in range(D_FF // LANES):
            chunk = w1_ref[:, 2 * LANES * j:2 * LANES * (j + 1)].astype(BF16)
            r = _dot(chunk, perm_ref[...])
            w1g_s[:, LANES * j:LANES * (j + 1)] = r[:, :LANES].astype(BF16)
            w1l_s[:, LANES * j:LANES * (j + 1)] = r[:, LANES:].astype(BF16)
        w2_s[...] = w2_ref[...].astype(BF16)

    @pl.when(hi > lo)
    def _():
        x = x_ref[...]
        acc = jnp.zeros(o_ref.shape, F32) + b2_ref[...]
        for c in range(0, D_FF, FF_CHUNK):
            hg = _dot(x, w1g_s[:, c:c + FF_CHUNK]) + b1g_ref[:, c:c + FF_CHUNK]
            hl = _dot(x, w1l_s[:, c:c + FF_CHUNK]) + b1l_ref[:, c:c + FF_CHUNK]
            hg = jnp.minimum(hg, SWIGLU_LIMIT)
            hl = jnp.clip(hl, -SWIGLU_LIMIT, SWIGLU_LIMIT)
            act = hg * jax.nn.sigmoid(SWIGLU_ALPHA * hg) * (hl + 1.0)
            acc = acc + _dot(act.astype(BF16), w2_s[c:c + FF_CHUNK, :])
        rows = blk_ref[w] * MOE_BM + lax.broadcasted_iota(jnp.int32, (MOE_BM, 1), 0)
        mine = (rows >= lo) & (rows < hi)
        y = acc.astype(o_ref.dtype)

        @pl.when(first_ref[w] == 1)
        def _():
            o_ref[...] = jnp.where(mine, y, jnp.zeros_like(y))

        @pl.when(first_ref[w] == 0)
        def _():
            o_ref[...] = jnp.where(mine, y, o_ref[...])


def _experts(x_sorted, meta, layer, w1, b1g, b1l, w2, b2):
    n_rows = x_sorted.shape[0]
    bm = MOE_BM
    n_items = meta[0].shape[0]
    o = np.arange(2 * LANES)[None, :]
    c = np.arange(2 * LANES)[:, None]
    perm = jnp.asarray(np.where(o < LANES, c == 2 * o, c == 2 * (o - LANES) + 1), BF16)
    wspec = lambda r, c: pl.BlockSpec((None, None, r, c), lambda w, blk, e, *_: (layer, e[w], 0, 0))
    bspec = lambda c: pl.BlockSpec((None, 1, c), lambda w, blk, e, *_: (e[w], 0, 0))
    grid_spec = pltpu.PrefetchScalarGridSpec(
        num_scalar_prefetch=6, grid=(n_items,),
        in_specs=[pl.BlockSpec((bm, D_MODEL), lambda w, blk, *_: (blk[w], 0)),
                  wspec(D_MODEL, 2 * D_FF), bspec(D_FF), bspec(D_FF), wspec(D_FF, D_MODEL), bspec(D_MODEL),
                  pl.BlockSpec(perm.shape, lambda w, *_: (0, 0))],
        out_specs=pl.BlockSpec((bm, D_MODEL), lambda w, blk, *_: (blk[w], 0)),
        scratch_shapes=[pltpu.VMEM((D_MODEL, D_FF), BF16), pltpu.VMEM((D_MODEL, D_FF), BF16),
                        pltpu.VMEM((D_FF, D_MODEL), BF16)])
    return pl.pallas_call(
        _expert_kernel, grid_spec=grid_spec,
        out_shape=jax.ShapeDtypeStruct((n_rows, D_MODEL), BF16),
        compiler_params=_cparams(("arbitrary",)), name="moe_experts",
    )(*meta, x_sorted, w1, b1g, b1l, w2, b2, perm)


def _combine_kernel(h_ref, y_ref, gate_ref, g_ref, b_ref, o_ref, *, alpha):
    gates = gate_ref[...]
    f = y_ref[0].astype(F32) * gates[:, 0:1]
    for k in range(1, TOP_K):
        f = f + y_ref[k].astype(F32) * gates[:, k:k + 1]
    o_ref[...] = _layer_norm(alpha * h_ref[...] + f, g_ref[...], b_ref[...])


def _combine_ln(h1, y_tok, gate_col, ln_g, ln_b, alpha, tok0, n_tok):
    tm = TM_OUT
    assert tok0 % tm == 0 and n_tok % tm == 0
    b0 = tok0 // tm
    return pl.pallas_call(
        functools.partial(_combine_kernel, alpha=alpha),
        grid=(n_tok // tm,),
        in_specs=[pl.BlockSpec((tm, D_MODEL), lambda i: (b0 + i, 0)),
                  pl.BlockSpec((TOP_K, tm, D_MODEL), lambda i: (0, b0 + i, 0)),
                  pl.BlockSpec((tm, LANES), lambda i: (b0 + i, 0)),
                  pl.BlockSpec((1, D_MODEL), lambda i: (0, 0)),
                  pl.BlockSpec((1, D_MODEL), lambda i: (0, 0))],
        out_specs=pl.BlockSpec((tm, D_MODEL), lambda i: (i, 0)),
        out_shape=jax.ShapeDtypeStruct((n_tok, D_MODEL), F32),
        compiler_params=_cparams(("parallel",)), name="combine_ln",
    )(h1, y_tok, gate_col, ln_g, ln_b)


def _routing(idx_t, rank_t, counts):
    t = idx_t.shape[1]
    n_asg = TOP_K * t
    bm = MOE_BM
    n_blocks = n_asg // bm
    n_items = n_blocks + N_EXPERTS - 1
    i32 = jnp.int32
    idx = idx_t[:TOP_K]
    tok = jnp.arange(t, dtype=i32)[None, :]
    skeys = lax.sort((idx * t + tok).reshape(-1))
    src_tok = skeys % t
    ends = jnp.cumsum(counts.astype(i32))
    starts = ends - counts.astype(i32)
    experts = jnp.arange(N_EXPERTS, dtype=i32)[:, None, None]
    pos = (jnp.sum(jnp.where(idx[None] == experts, starts[:, None, None], 0), axis=0) + rank_t[:TOP_K]).reshape(-1)
    first_tile = starts // bm
    ntiles = jnp.where(ends > starts, (ends + bm - 1) // bm - first_tile, 0)
    cum = jnp.cumsum(ntiles)
    base = cum - ntiles
    total = cum[-1]
    w = jnp.arange(n_items, dtype=i32)
    valid = w < total
    e_w = jnp.minimum(jnp.sum((cum[None, :] <= w[:, None]).astype(i32), axis=1), N_EXPERTS - 1)
    e_last = jnp.sum(jnp.where(w == total - 1, e_w, 0))
    e_w = jnp.where(valid, e_w, e_last)
    onehot_e = e_w[:, None] == jnp.arange(N_EXPERTS, dtype=i32)[None, :]
    at_e = lambda table: jnp.sum(jnp.where(onehot_e, table[None, :], 0), axis=1)
    blk = jnp.where(valid, at_e(first_tile) + (w - at_e(base)), n_blocks - 1).astype(i32)
    lo = jnp.where(valid, jnp.maximum(at_e(starts), blk * bm), 0).astype(i32)
    hi = jnp.where(valid, jnp.minimum(at_e(ends), (blk + 1) * bm), 0).astype(i32)
    one = jnp.ones((1,), i32)
    first = jnp.concatenate([one, (blk[1:] != blk[:-1]).astype(i32)])
    new_e = jnp.concatenate([one, (e_w[1:] != e_w[:-1]).astype(i32)])
    return src_tok, pos, (blk, e_w, lo, hi, first, new_e)


def _prep_layer(i, w_in, conv_w, conv_b, a_log, dt_bias, w_pool, pool_scale, rpb, w_out, w_router, b_router,
                w1, b1, w2, b2, d_skip, ssd_norm_g, ln1_g, ln1_b, ln2_g, ln2_b):
    c_x = POOL_W + SSD_INNER + SSD_XBC
    c_dt = c_x + DT_COLS
    w = w_in[i]
    w_dt = w[:, c_x:c_dt]
    w_q, w_k, w_v = (w[:, c_dt + j * ATT_W:c_dt + (j + 1) * ATT_W] for j in range(3))
    pad = LANES - DT_REP * DT_COLS
    w_all = jnp.concatenate([w[:, :c_x], w_k] + [w_dt] * DT_REP + [jnp.zeros((D_MODEL, pad), F32)],
                            axis=1).astype(BF16)
    w_t = jnp.concatenate([w_q, w_v, w_dt], axis=1).T.astype(BF16)
    packed_row = lambda v: jnp.concatenate([v.reshape(1, DT_COLS)] * DT_REP + [jnp.zeros((1, pad), F32)], axis=1)
    w_bd = jnp.zeros((POOL_W, POOL_W), F32)
    for g in range(POOL_GROUPS):
        sl = slice(g * POOL_GDIM, (g + 1) * POOL_GDIM)
        w_bd = w_bd.at[sl, sl].set(w_pool[i, g])
    conv_w_pad = jnp.concatenate([conv_w[i], jnp.zeros((HALO - SSD_CONV, SSD_XBC), F32)], axis=0)
    return dict(
        layer=i, w1=w1, w2=w2,
        w_all=w_all, w_t=w_t, w_bd=w_bd.astype(BF16), pool_scale=pool_scale[i].reshape(1, POOL_W),
        conv_w=conv_w_pad, conv_b=conv_b[i].reshape(1, SSD_XBC),
        dt_bias_row=packed_row(dt_bias[i]), dt_bias_col=dt_bias[i].reshape(DT_COLS, 1),
        alog_row=packed_row(a_log[i]), alog_col=a_log[i].reshape(DT_COLS, 1),
        bias_tab=_att_bias_table(rpb[i]),
        w_out=w_out[i].astype(BF16), dskip=jnp.repeat(d_skip[i], SSD_HDIM).reshape(1, SSD_INNER),
        norm_g=ssd_norm_g[i].reshape(1, SSD_INNER), ln1_g=ln1_g[i].reshape(1, D_MODEL), ln1_b=ln1_b[i].reshape(1, D_MODEL),
        w_router_t=jnp.concatenate(_split3(w_router[i].T)[:2], axis=0), b_router=b_router[i].reshape(N_EXPERTS, 1),
        b1g=b1[i, :, 0::2].reshape(N_EXPERTS, 1, D_FF), b1l=b1[i, :, 1::2].reshape(N_EXPERTS, 1, D_FF),
        b2=b2[i].reshape(N_EXPERTS, 1, D_MODEL),
        ln2_g=ln2_g[i].reshape(1, D_MODEL), ln2_b=ln2_b[i].reshape(1, D_MODEL))


def _encoder_layer(srcs, p, seq_groups, alpha, split_out):
    t = sum(a.shape[0] for a in srcs)
    seq_lens = tuple(s for tok0, n, s in seq_groups for _ in range(n))
    u, z, xbc, k, dt, qt, vt, dtt = _in_proj(srcs, p['w_all'], p['w_t'])
    pool_out = _pool_mixer(u, p['w_bd'], p['pool_scale'], seq_lens)
    xs, xcf, xcb, c, bt = _conv_silu(xbc, dt, p['conv_w'], p['conv_b'], p['dt_bias_row'], seq_lens)
    y_f, y_b = _ssd_scan(xcf, xcb, c, bt, dt, dtt, p['dt_bias_row'], p['dt_bias_col'], p['alog_row'], p['alog_col'],
                         seq_lens)
    atts = [_attention(qt, k, vt, p['bias_tab'], tok0, n, s) for tok0, n, s in seq_groups]
    h1, h1b, idx_t, rank_t, gate_col, cnt = _out_proj_router(
        srcs, pool_out, y_f, y_b, xs, z, atts, p['w_out'], p['dskip'], p['norm_g'], p['ln1_g'], p['ln1_b'],
        p['w_router_t'], p['b_router'], alpha)
    src_tok, pos, meta = _routing(idx_t, rank_t, cnt[:, 0])
    x_sorted = jnp.take(h1b, src_tok, axis=0, mode='clip')
    y_sorted = _experts(x_sorted, meta, p['layer'], p['w1'], p['b1g'], p['b1l'], p['w2'], p['b2'])
    y_tok = jnp.take(y_sorted, pos, axis=0, mode='clip').reshape(TOP_K, t, D_MODEL)
    ranges = [(tok0, n * s) for tok0, n, s in seq_groups] if split_out else [(0, t)]
    return [_combine_ln(h1, y_tok, gate_col, p['ln2_g'], p['ln2_b'], alpha, tok0, n_tok) for tok0, n_tok in ranges]


def kernel(x_prompt, x_sample, w_in, conv_w, conv_b, a_log, dt_bias, d_skip, ssd_norm_g, w_pool, pool_scale, rpb, w_out,
           ln1_g, ln1_b, w_router, b_router, w1, b1, w2, b2, ln2_g, ln2_b):
    depth = w_in.shape[0]
    alpha = (2 * depth) ** 0.25
    bp, sp, _ = x_prompt.shape
    bs, ss, _ = x_sample.shape
    seq_groups = ((0, bp, sp), (bp * sp, bs, ss))
    outs = [x_prompt.reshape(bp * sp, D_MODEL), x_sample.reshape(bs * ss, D_MODEL)]
    for i in range(depth):
        p = _prep_layer(i, w_in, conv_w, conv_b, a_log, dt_bias, w_pool, pool_scale, rpb, w_out, w_router, b_router,
                        w1, b1, w2, b2, d_skip, ssd_norm_g, ln1_g, ln1_b, ln2_g, ln2_b)
        outs = _encoder_layer(outs, p, seq_groups, alpha, split_out=(i == depth - 1))
    return (outs[0].reshape(bp, sp, D_MODEL), outs[1].reshape(bs, ss, D_MODEL))
```

```python
import functools

import numpy as np
import jax
import jax.numpy as jnp
from jax import lax
from jax.experimental import pallas as pl
from jax.experimental.pallas import tpu as pltpu

F32 = jnp.float32
BF16 = jnp.bfloat16

D_MODEL = 1024
POOL_GROUPS = 4
POOL_GDIM = 64
POOL_W = POOL_GROUPS * POOL_GDIM
POOL_WINDOWS = (2, 4, 8, 16)
SSD_HEADS = 8
SSD_HDIM = 64
SSD_INNER = SSD_HEADS * SSD_HDIM
SSD_GROUPS = 2
SSD_STATE = 128
SSD_CONV = 5
SSD_CHUNK = 128
SSD_BC = 2 * SSD_GROUPS * SSD_STATE
SSD_XBC = SSD_INNER + SSD_BC
SSD_GW = SSD_INNER // SSD_GROUPS
ATT_HEADS = 4
ATT_HDIM = 64
ATT_W = ATT_HEADS * ATT_HDIM
GRID_W = 64
NA_ROWS = 8
NA_COLS = 16
N_EXPERTS = 32
TOP_K = 4
D_FF = D_MODEL
SWIGLU_LIMIT = 7.0
SWIGLU_ALPHA = 1.702
LN_EPS = 1e-5
RMS_EPS = 1e-5

LANES = 128
DT_COLS = 2 * SSD_HEADS
DT_REP = 3
HALO = 8
NEG_BIG = -1e30
WIN_ROWS = NA_ROWS + 2
WIN_KEYS = WIN_ROWS * GRID_W

TM_PROJ = 512
TM_OUT = 512
SSD_CHUNKS_PER_STEP = 8
TS_SEQ = 512
ATT_PAIRS_PER_STEP = 8
MOE_BM = 512
FF_CHUNK = 1024
VMEM_LIMIT = 56 * 1024 * 1024


def _cparams(sem):
    return pltpu.CompilerParams(dimension_semantics=sem, vmem_limit_bytes=VMEM_LIMIT)


def _dot(a, b):
    return jnp.dot(a, b, preferred_element_type=F32)


def _dot_nt(a, b, precision=None):
    return lax.dot_general(a, b, (((1,), (1,)), ((), ())), preferred_element_type=F32, precision=precision)


def _softplus(v):
    return jnp.maximum(v, 0.0) + jnp.log1p(jnp.exp(-jnp.abs(v)))


def _split3(v):
    hi = v.astype(BF16)
    r1 = v - hi.astype(F32)
    mid = r1.astype(BF16)
    lo = (r1 - mid.astype(F32)).astype(BF16)
    return hi, mid, lo


def _pack3(v):
    hi, mid, lo = _split3(v)
    lane = lax.broadcasted_iota(jnp.int32, v.shape, 1)
    zero = jnp.zeros(v.shape, BF16)
    return jnp.where(lane < DT_COLS, hi, jnp.where(lane < 2 * DT_COLS, mid, jnp.where(lane < 3 * DT_COLS, lo, zero)))


def _expand_matrix(off, width):
    r = np.arange(LANES)[:, None]
    c = np.arange(SSD_HEADS * width)[None, :]
    m = (r < DT_REP * DT_COLS) & ((r % DT_COLS) == off + c // width)
    return jnp.asarray(m, BF16)


def _token_tile_specs(srcs, tm):
    specs, bounds, first = [], [], 0
    for a in srcs:
        n = a.shape[0] // tm
        assert a.shape[0] % tm == 0
        specs.append(pl.BlockSpec((tm, a.shape[1]), lambda i, first=first, n=n: (jnp.clip(i - first, 0, n - 1), 0)))
        first += n
        bounds.append(first)
    return specs, tuple(bounds)


def _pick_source(refs, bounds):
    x = refs[-1][...]
    for ref, end in zip(reversed(refs[:-1]), reversed(bounds[:-1])):
        x = jnp.where(pl.program_id(0) < end, ref[...], x)
    return x


def _inproj_kernel(*refs, bounds):
    n_src = len(bounds)
    w_ref, wt_ref, u_ref, z_ref, xbc_ref, k_ref, dt_ref, qt_ref, vt_ref, dtt_ref = refs[n_src:]
    xb = _pick_source(refs[:n_src], bounds).astype(BF16)
    c0 = 0
    for ref, width in ((u_ref, POOL_W), (z_ref, SSD_INNER), (xbc_ref, SSD_XBC), (k_ref, ATT_W), (dt_ref, LANES)):
        ref[...] = _dot(xb, w_ref[:, c0:c0 + width]).astype(ref.dtype)
        c0 += width
    tr = _dot_nt(wt_ref[...], xb)
    qt_ref[...] = (tr[:ATT_W] * ATT_HDIM ** -0.5).astype(BF16)
    vt_ref[...] = tr[ATT_W:2 * ATT_W].astype(BF16)
    dtt_ref[...] = tr[2 * ATT_W:]


def _in_proj(srcs, w_all, w_t):
    t = sum(a.shape[0] for a in srcs)
    tm = TM_PROJ
    row = lambda w: pl.BlockSpec((tm, w), lambda i: (i, 0))
    col = lambda r: pl.BlockSpec((r, tm), lambda i: (0, i))
    full = lambda a: pl.BlockSpec(a.shape, lambda i: (0,) * a.ndim)
    src_specs, bounds = _token_tile_specs(srcs, tm)
    return pl.pallas_call(
        functools.partial(_inproj_kernel, bounds=bounds),
        grid=(t // tm,),
        in_specs=src_specs + [full(w_all), full(w_t)],
        out_specs=[row(POOL_W), row(SSD_INNER), row(SSD_XBC), row(ATT_W), row(LANES),
                   col(ATT_W), col(ATT_W), col(DT_COLS)],
        out_shape=[jax.ShapeDtypeStruct((t, POOL_W), F32), jax.ShapeDtypeStruct((t, SSD_INNER), F32),
                   jax.ShapeDtypeStruct((t, SSD_XBC), F32), jax.ShapeDtypeStruct((t, ATT_W), BF16),
                   jax.ShapeDtypeStruct((t, LANES), F32), jax.ShapeDtypeStruct((ATT_W, t), BF16),
                   jax.ShapeDtypeStruct((ATT_W, t), BF16), jax.ShapeDtypeStruct((DT_COLS, t), F32)],
        compiler_params=_cparams(("parallel",)),
        name="in_proj",
    )(*srcs, w_all, w_t)


def _seq_tables(seq_lens, tile):
    pos, length = [], []
    for s in seq_lens:
        assert s % tile == 0
        for p in range(0, s, tile):
            pos.append(p)
            length.append(s)
    return np.asarray(pos, np.int32), np.asarray(length, np.int32)


def _halo_specs(ts, width, n_tok):
    hb = ts // HALO
    last = n_tok // HALO - 1
    cur = pl.BlockSpec((ts, width), lambda i, *_: (i, 0))
    prev = pl.BlockSpec((HALO, width), lambda i, *_: (jnp.maximum(i * hb - 1, 0), 0))
    nxt = pl.BlockSpec((HALO, width), lambda i, *_: (jnp.minimum((i + 1) * hb, last), 0))
    return prev, cur, nxt


def _extended(prev_ref, cur_ref, next_ref, pos0, slen, ts):
    prev = jnp.where(pos0 > 0, prev_ref[...], 0.0)
    nxt = jnp.where(pos0 + ts < slen, next_ref[...], 0.0)
    return jnp.concatenate([prev, cur_ref[...], nxt], axis=0)


def _rows_from(ext, k, ts):
    n = ext.shape[0]
    shift = (n - (HALO + k)) % n
    rolled = ext if shift == 0 else pltpu.roll(ext, shift, 0)
    return rolled[:ts]


def _pool_kernel(pos_ref, len_ref, prev_ref, cur_ref, next_ref, w_ref, scale_ref, o_ref, *, ts):
    i = pl.program_id(0)
    pos0 = pos_ref[i]
    slen = len_ref[i]
    ext = _extended(prev_ref, cur_ref, next_ref, pos0, slen, ts)
    n = ext.shape[0]

    def pair_sum(a, k):
        return a + pltpu.roll(a, n - k, 0)

    a2 = pair_sum(ext, 1)
    a4 = pair_sum(a2, 2)
    a8 = pair_sum(a4, 4)
    a16 = pair_sum(a8, 8)
    sums = [_rows_from(a, -w // 2, ts) for a, w in zip((a2, a4, a8, a16), POOL_WINDOWS)]

    lane = lax.broadcasted_iota(jnp.int32, (ts, POOL_W), 1)
    grp = lane // POOL_GDIM
    wsum = jnp.where(grp == 0, sums[0], jnp.where(grp == 1, sums[1], jnp.where(grp == 2, sums[2], sums[3])))
    half = jnp.where(grp == 0, 1, jnp.where(grp == 1, 2, jnp.where(grp == 2, 4, 8)))
    tpos = pos0 + lax.broadcasted_iota(jnp.int32, (ts, POOL_W), 0)
    cnt = (jnp.minimum(tpos + half, slen) - jnp.maximum(tpos - half, 0)).astype(F32)
    pooled = wsum / cnt - cur_ref[...]
    o_ref[...] = _dot(pooled.astype(BF16), w_ref[...]) * scale_ref[...]


def _pool_mixer(u, w_bd, scale, seq_lens):
    t = u.shape[0]
    ts = TS_SEQ
    pos, length = _seq_tables(seq_lens, ts)
    prev, cur, nxt = _halo_specs(ts, POOL_W, t)
    grid_spec = pltpu.PrefetchScalarGridSpec(
        num_scalar_prefetch=2, grid=(t // ts,),
        in_specs=[prev, cur, nxt,
                  pl.BlockSpec((POOL_W, POOL_W), lambda i, *_: (0, 0)),
                  pl.BlockSpec((1, POOL_W), lambda i, *_: (0, 0))],
        out_specs=pl.BlockSpec((ts, POOL_W), lambda i, *_: (i, 0)))
    return pl.pallas_call(
        functools.partial(_pool_kernel, ts=ts), grid_spec=grid_spec,
        out_shape=jax.ShapeDtypeStruct((t, POOL_W), F32),
        compiler_params=_cparams(("parallel",)), name="pool_mixer",
    )(pos, length, u, u, u, w_bd, scale)


def _conv_kernel(pos_ref, len_ref, prev_ref, cur_ref, next_ref, dt_ref, w_ref, b_ref, dtb_ref, ef_ref, eb_ref,
                 xs_ref, xcf_ref, xcb_ref, c_ref, bt_ref, *, ts):
    i = pl.program_id(0)
    ext = _extended(prev_ref, cur_ref, next_ref, pos_ref[i], len_ref[i], ts)
    acc = jnp.zeros((ts, SSD_XBC), F32) + b_ref[...]
    for j in range(SSD_CONV):
        acc = acc + _rows_from(ext, j - SSD_CONV // 2, ts) * w_ref[j:j + 1, :]
    act = acc * jax.nn.sigmoid(acc)
    xs = act[:, :SSD_INNER]
    xs_ref[...] = xs
    dt_pk = _pack3(_softplus(dt_ref[...] + dtb_ref[...]))
    xcf_ref[...] = (xs * _dot(dt_pk, ef_ref[...])).astype(BF16)
    xcb_ref[...] = (xs * _dot(dt_pk, eb_ref[...])).astype(BF16)
    nb = SSD_GROUPS * SSD_STATE
    bt_ref[...] = act[:, SSD_INNER:SSD_INNER + nb].T.astype(BF16)
    c_ref[...] = act[:, SSD_INNER + nb:].astype(BF16)


def _conv_silu(xbc, dt, conv_w, conv_b, dt_bias_row, seq_lens):
    t = xbc.shape[0]
    ts = TS_SEQ
    pos, length = _seq_tables(seq_lens, ts)
    prev, cur, nxt = _halo_specs(ts, SSD_XBC, t)
    e_f = _expand_matrix(0, SSD_HDIM)
    e_b = _expand_matrix(SSD_HEADS, SSD_HDIM)
    nb = SSD_GROUPS * SSD_STATE
    const = lambda a: pl.BlockSpec(a.shape, lambda i, *_: (0, 0))
    row = lambda w: pl.BlockSpec((ts, w), lambda i, *_: (i, 0))
    grid_spec = pltpu.PrefetchScalarGridSpec(
        num_scalar_prefetch=2, grid=(t // ts,),
        in_specs=[prev, cur, nxt, row(LANES), const(conv_w), const(conv_b), const(dt_bias_row), const(e_f), const(e_b)],
        out_specs=[row(SSD_INNER), row(SSD_INNER), row(SSD_INNER), row(nb),
                   pl.BlockSpec((nb, ts), lambda i, *_: (0, i))])
    return pl.pallas_call(
        functools.partial(_conv_kernel, ts=ts), grid_spec=grid_spec,
        out_shape=[jax.ShapeDtypeStruct((t, SSD_INNER), F32), jax.ShapeDtypeStruct((t, SSD_INNER), BF16),
                   jax.ShapeDtypeStruct((t, SSD_INNER), BF16), jax.ShapeDtypeStruct((t, nb), BF16),
                   jax.ShapeDtypeStruct((nb, t), BF16)],
        compiler_params=_cparams(("parallel",)), name="conv_silu",
    )(pos, length, xbc, xbc, xbc, dt, conv_w, conv_b, dt_bias_row, e_f, e_b)


def _ssd_chunk(xc, c, bt, dt_full, dtt, aneg_full, aneg_col, e128, state_ref, reset, reverse, off):
    L = SSD_CHUNK
    ii = lax.broadcasted_iota(jnp.int32, (L, L), 0)
    jj = lax.broadcasted_iota(jnp.int32, (L, L), 1)
    tri = (jj >= ii) if reverse else (jj <= ii)
    tri_t = (ii >= jj) if reverse else (ii <= jj)
    tri_b = tri.astype(BF16)
    tri_tb = tri_t.astype(BF16)
    acs = sum(_dot(tri_b, part) for part in _split3(dt_full * aneg_full))
    acs_row = sum(_dot(part, tri_tb) for part in _split3(dtt * aneg_col))
    edge = 0 if reverse else L - 1
    acs_b = _dot(_pack3(acs), e128)
    lane = lax.broadcasted_iota(jnp.int32, (L, LANES), 1)
    low = lane < SSD_HDIM
    acs_e = jnp.concatenate(
        [jnp.where(low, acs_b[:, (2 * j) * LANES:(2 * j + 1) * LANES], acs_b[:, (2 * j + 1) * LANES:(2 * j + 2) * LANES])
         for j in range(SSD_HEADS // 2)], axis=1)
    tot_e = acs_e[edge:edge + 1, :]
    exp_acs = jnp.exp(acs_e)
    to_end = jnp.exp(tot_e - acs_e)
    chunk_decay = jnp.exp(tot_e)
    xw = (xc.astype(F32) * to_end).astype(BF16)
    zero_b = jnp.zeros((L, LANES), BF16)

    ys = []
    for g in range(SSD_GROUPS):
        gs = slice(g * SSD_GW, (g + 1) * SSD_GW)
        c_g = c[:, g * SSD_STATE:(g + 1) * SSD_STATE]
        bt_g = bt[g * SSD_STATE:(g + 1) * SSD_STATE, :]
        cb = _dot(c_g, bt_g)
        prev = state_ref[g] if reset is None else jnp.where(reset, 0.0, state_ref[g])
        y_off = _dot(c_g, prev.astype(BF16)) * exp_acs[:, gs]
        state_ref[g] = prev * chunk_decay[:, gs] + _dot(bt_g, xw[:, gs])
        for pair in range(SSD_GW // LANES):
            t0 = g * SSD_GW + pair * LANES
            tile = xc[:, t0:t0 + LANES]
            halves = (jnp.where(low, tile, zero_b), jnp.where(low, zero_b, tile))
            y_pair = y_off[:, pair * LANES:(pair + 1) * LANES]
            for s in range(2):
                h = t0 // SSD_HDIM + s
                seg = acs_b[:, h * LANES:(h + 1) * LANES] - acs_row[off + h:off + h + 1, :]
                m = (cb * jnp.exp(jnp.where(tri, seg, NEG_BIG))).astype(BF16)
                y_pair = y_pair + _dot(m, halves[s])
            ys.append(y_pair)
    return jnp.concatenate(ys, axis=1)


def _ssd_kernel(mir_ref, start_ref, xc_f, c_f, bt_f, dt_f, dtt_f, xc_b, c_b, bt_b, dt_b, dtt_b,
                bias_row, bias_col, alog_row, alog_col, ef_ref, eb_ref, yf_ref, yb_ref, st_f, st_b):
    step = pl.program_id(0)
    L = SSD_CHUNK
    first = start_ref[step] == 1
    aneg_full = -jnp.exp(alog_row[...])
    aneg_col = -jnp.exp(alog_col[...])
    dirs = ((xc_f, c_f, bt_f, dt_f, dtt_f, ef_ref, yf_ref, st_f, False, 0),
            (xc_b, c_b, bt_b, dt_b, dtt_b, eb_ref, yb_ref, st_b, True, SSD_HEADS))
    for n in range(SSD_CHUNKS_PER_STEP):
        for xc, c, bt, dt, dtt, e_ref, y_ref, st, reverse, off in dirs:
            sub = SSD_CHUNKS_PER_STEP - 1 - n if reverse else n
            rows = slice(sub * L, (sub + 1) * L)
            dtc = _softplus(dt[rows, :] + bias_row[...])
            dtr = _softplus(dtt[:, rows] + bias_col[...])
            reset = first if n == 0 else None
            y_ref[rows, :] = _ssd_chunk(xc[rows, :], c[rows, :], bt[:, rows], dtc, dtr, aneg_full, aneg_col, e_ref[...],
                                        st, reset, reverse, off)


def _ssd_tables(seq_lens, block):
    mirror, start = [], []
    c0 = 0
    for s in seq_lens:
        assert s % block == 0
        n = s // block
        for c in range(n):
            mirror.append(c0 + n - 1 - c)
            start.append(1 if c == 0 else 0)
        c0 += n
    return np.asarray(mirror, np.int32), np.asarray(start, np.int32)


def _ssd_scan(xcf, xcb, c, bt, dt, dtt, dt_bias_row, dt_bias_col, alog_row, alog_col, seq_lens):
    t = xcf.shape[0]
    L = SSD_CHUNK * SSD_CHUNKS_PER_STEP
    nb = SSD_GROUPS * SSD_STATE
    mirror, start = _ssd_tables(seq_lens, L)
    e_f = _expand_matrix(0, LANES)
    e_b = _expand_matrix(SSD_HEADS, LANES)
    fw = lambda w: pl.BlockSpec((L, w), lambda s, m, st: (s, 0))
    bw = lambda w: pl.BlockSpec((L, w), lambda s, m, st: (m[s], 0))
    fw_t = lambda r: pl.BlockSpec((r, L), lambda s, m, st: (0, s))
    bw_t = lambda r: pl.BlockSpec((r, L), lambda s, m, st: (0, m[s]))
    const = lambda a: pl.BlockSpec(a.shape, lambda s, m, st: (0, 0))
    consts = (dt_bias_row, dt_bias_col, alog_row, alog_col, e_f, e_b)
    grid_spec = pltpu.PrefetchScalarGridSpec(
        num_scalar_prefetch=2, grid=(t // L,),
        in_specs=[fw(SSD_INNER), fw(nb), fw_t(nb), fw(LANES), fw_t(DT_COLS),
                  bw(SSD_INNER), bw(nb), bw_t(nb), bw(LANES), bw_t(DT_COLS)] + [const(a) for a in consts],
        out_specs=[fw(SSD_INNER), bw(SSD_INNER)],
        scratch_shapes=[pltpu.VMEM((SSD_GROUPS, SSD_STATE, SSD_GW), F32),
                        pltpu.VMEM((SSD_GROUPS, SSD_STATE, SSD_GW), F32)])
    return pl.pallas_call(
        _ssd_kernel, grid_spec=grid_spec,
        out_shape=[jax.ShapeDtypeStruct((t, SSD_INNER), F32), jax.ShapeDtypeStruct((t, SSD_INNER), F32)],
        compiler_params=_cparams(("arbitrary",)), name="ssd_scan",
    )(mirror, start, xcf, c, bt, dt, dtt, xcb, c, bt, dt, dtt, *consts)


ATT_VARIANTS = 5
_ATT_RS_OFF = ((0, -1), (-2, -3), (-4, -4), (-4, -5), (-6, -7))


def _att_bias_table(rpb):
    kc = np.arange(GRID_W)[:, None]
    c = np.arange(GRID_W)[None, :]
    cs = np.clip(c - NA_COLS // 2, 0, GRID_W - NA_COLS)
    valid = (kc >= cs) & (kc < cs + NA_COLS)
    d = kc - c + NA_COLS - 1
    onehot = (d[None] == np.arange(2 * NA_COLS - 1)[:, None, None]) & valid[None]
    tt = jnp.einsum('hrd,dkc->hrkc', rpb.astype(F32), jnp.asarray(onehot, F32),
                    precision=jax.lax.Precision.HIGHEST)
    tt = jnp.where(valid, tt, NEG_BIG)
    neg = jnp.full((ATT_HEADS, GRID_W, GRID_W), NEG_BIG, F32)
    variants = []
    for var in range(ATT_VARIANTS):
        halves = []
        for dq in range(2):
            tiles = []
            for i in range(WIN_ROWS):
                delta = i - 2 * var - dq
                inside = _ATT_RS_OFF[var][dq] <= delta < _ATT_RS_OFF[var][dq] + NA_ROWS
                tiles.append(tt[:, delta + NA_ROWS - 1] if inside else neg)
            halves.append(jnp.concatenate(tiles, axis=1))
        variants.append(jnp.concatenate(halves, axis=2))
    return jnp.stack(variants)


def _att_kernel(qt_ref, k_ref, vt_ref, bias_ref, o_ref, *, rows, pb):
    j = pl.program_id(1)
    sub = lax.broadcasted_iota(jnp.int32, (LANES, LANES), 0)
    low = sub < ATT_HDIM
    zero = jnp.zeros((LANES, LANES), BF16)
    edge = NA_ROWS // 2
    for pp in range(pb):
        r = (j * pb + pp) * 2
        ws = jnp.clip(r - edge, 0, rows - WIN_ROWS)
        var = jnp.where(r < edge, r // 2, jnp.where(r >= rows - edge, (r - (rows - edge)) // 2 + 3, 2))
        k0 = pl.multiple_of(ws * GRID_W, LANES)
        kwin = k_ref[pl.ds(k0, WIN_KEYS), :]
        vtw = vt_ref[:, pl.ds(k0, WIN_KEYS)]
        qt = qt_ref[:, pp * LANES:(pp + 1) * LANES]
        outs = []
        for jp in range(ATT_HEADS // 2):
            qpair = qt[jp * LANES:(jp + 1) * LANES, :]
            qboth = jnp.concatenate([jnp.where(low, qpair, zero), jnp.where(low, zero, qpair)], axis=1)
            s2 = _dot(kwin[:, jp * LANES:(jp + 1) * LANES], qboth)
            for hh in range(2):
                h = 2 * jp + hh
                s = s2[:, hh * LANES:(hh + 1) * LANES] + bias_ref[var, h]
                m = jnp.max(s, axis=0, keepdims=True)
                p = jnp.exp(s - m)
                den = jnp.sum(p, axis=0, keepdims=True)
                outs.append(_dot(vtw[h * ATT_HDIM:(h + 1) * ATT_HDIM, :], p.astype(BF16)) / den)
        ot = jnp.concatenate(outs, axis=0)
        o_ref[pp * LANES:(pp + 1) * LANES, :] = ot.T.astype(o_ref.dtype)


def _attention(qt, k, vt, bias_tab, tok0, n_seq, seq_len):
    rows = seq_len // GRID_W
    pb = ATT_PAIRS_PER_STEP
    assert rows >= WIN_ROWS and rows % (2 * pb) == 0 and tok0 % seq_len == 0 and 2 * GRID_W == LANES
    steps = rows // (2 * pb)
    s0 = tok0 // seq_len
    qb0 = tok0 // (pb * LANES)
    return pl.pallas_call(
        functools.partial(_att_kernel, rows=rows, pb=pb),
        grid=(n_seq, steps),
        in_specs=[pl.BlockSpec((ATT_W, pb * LANES), lambda b, j: (0, qb0 + b * steps + j)),
                  pl.BlockSpec((seq_len, ATT_W), lambda b, j: (s0 + b, 0)),
                  pl.BlockSpec((ATT_W, seq_len), lambda b, j: (0, s0 + b)),
                  pl.BlockSpec(bias_tab.shape, lambda b, j: (0, 0, 0, 0))],
        out_specs=pl.BlockSpec((pb * LANES, ATT_W), lambda b, j: (b * steps + j, 0)),
        out_shape=jax.ShapeDtypeStruct((n_seq * seq_len, ATT_W), BF16),
        compiler_params=_cparams(("parallel", "parallel")), name="nbr_attention",
    )(qt, k, vt, bias_tab)


def _layer_norm(x, g, b):
    mu = jnp.mean(x, axis=-1, keepdims=True)
    xc = x - mu
    var = jnp.mean(xc * xc, axis=-1, keepdims=True)
    return xc * lax.rsqrt(var + LN_EPS) * g + b


def _outproj_kernel(*refs, alpha, bounds, att_bounds):
    n_src, n_att = len(bounds), len(att_bounds)
    h_refs, att_refs = refs[:n_src], refs[n_src:n_src + n_att]
    (pool_ref, yf_ref, yb_ref, xs_ref, z_ref, wo_ref, dskip_ref, ng_ref, g_ref, b_ref, wr_ref, br_ref,
     h1_ref, h1b_ref, idx_ref, rank_ref, gate_ref, cnt_ref) = refs[n_src + n_att:]
    h_in = _pick_source(h_refs, bounds)
    att = _pick_source(att_refs, att_bounds)

    @pl.when(pl.program_id(0) == 0)
    def _():
        cnt_ref[...] = jnp.zeros(cnt_ref.shape, F32)

    y = yf_ref[...] + yb_ref[...] + xs_ref[...] * dskip_ref[...]
    z = z_ref[...]
    y = y * (z * jax.nn.sigmoid(z))
    parts = []
    for g in range(SSD_GROUPS):
        yg = y[:, g * SSD_GW:(g + 1) * SSD_GW]
        parts.append(yg * lax.rsqrt(jnp.mean(yg * yg, axis=-1, keepdims=True) + RMS_EPS))
    ssd = (jnp.concatenate(parts, axis=1) * ng_ref[...]).astype(BF16)
    m = (_dot(pool_ref[...].astype(BF16), wo_ref[0:POOL_W, :])
         + _dot(ssd, wo_ref[POOL_W:POOL_W + SSD_INNER, :])
         + _dot(att, wo_ref[POOL_W + SSD_INNER:, :]))
    h1 = _layer_norm(alpha * h_in + m, g_ref[...], b_ref[...])
    h1_ref[...] = h1
    h1_hi = h1.astype(BF16)
    h1b_ref[...] = h1_hi
    h1_lo = (h1 - h1_hi.astype(F32)).astype(BF16)

    hh = _dot_nt(wr_ref[...], h1_hi)
    logits = hh[:N_EXPERTS] + hh[N_EXPERTS:] + _dot_nt(wr_ref[0:N_EXPERTS, :], h1_lo) + br_ref[...]
    eidx = lax.broadcasted_iota(jnp.int32, logits.shape, 0)
    vals, idxs = [], []
    cur = logits
    for _ in range(TOP_K):
        mx = jnp.max(cur, axis=0, keepdims=True)
        ix = jnp.min(jnp.where(cur == mx, eidx, N_EXPERTS), axis=0, keepdims=True)
        vals.append(mx)
        idxs.append(ix)
        cur = jnp.where(eidx == ix, -jnp.inf, cur)
    es = [jnp.exp(vv - vals[0]) for vv in vals]
    den = es[0] + es[1] + es[2] + es[3]
    tm = logits.shape[1]
    idx_ref[...] = jnp.concatenate(idxs + [jnp.zeros((8 - TOP_K, tm), jnp.int32)], axis=0)

    ti = lax.broadcasted_iota(jnp.int32, (LANES, LANES), 0)
    tj = lax.broadcasted_iota(jnp.int32, (LANES, LANES), 1)
    before = (ti < tj).astype(BF16)
    onehots = [(eidx == ix).astype(F32) for ix in idxs]
    sel = onehots[0] + onehots[1] + onehots[2] + onehots[3]
    base = cnt_ref[...]
    ranks = [[] for _ in range(TOP_K)]
    for q in range(tm // LANES):
        ls = slice(q * LANES, (q + 1) * LANES)
        prefix = _dot(sel[:, ls].astype(BF16), before) + base
        for k in range(TOP_K):
            ranks[k].append(jnp.sum(onehots[k][:, ls] * prefix, axis=0, keepdims=True))
        base = base + jnp.sum(sel[:, ls], axis=1, keepdims=True)
    cnt_ref[...] = base
    rank_rows = [jnp.concatenate(r, axis=1) for r in ranks] + [jnp.zeros((8 - TOP_K, tm), F32)]
    rank_ref[...] = jnp.concatenate(rank_rows, axis=0).astype(jnp.int32)

    gates_t = jnp.concatenate([e / den for e in es] + [jnp.zeros((LANES - TOP_K, tm), F32)], axis=0)
    gate_ref[...] = gates_t.T


def _out_proj_router(srcs, pool_out, y_f, y_b, xs, z, atts, w_out, dskip, norm_g, ln_g, ln_b, w_router_t, b_router, alpha):
    t = pool_out.shape[0]
    tm = TM_OUT
    row = lambda w: pl.BlockSpec((tm, w), lambda i: (i, 0))
    full = lambda a: pl.BlockSpec(a.shape, lambda i: (0,) * a.ndim)
    consts = (w_out, dskip, norm_g, ln_g, ln_b, w_router_t, b_router)
    src_specs, bounds = _token_tile_specs(srcs, tm)
    att_specs, att_bounds = _token_tile_specs(atts, tm)
    return pl.pallas_call(
        functools.partial(_outproj_kernel, alpha=alpha, bounds=bounds, att_bounds=att_bounds),
        grid=(t // tm,),
        in_specs=src_specs + att_specs + [row(POOL_W), row(SSD_INNER), row(SSD_INNER), row(SSD_INNER),
                                          row(SSD_INNER)] + [full(a) for a in consts],
        out_specs=[row(D_MODEL), row(D_MODEL), pl.BlockSpec((8, tm), lambda i: (0, i)),
                   pl.BlockSpec((8, tm), lambda i: (0, i)), row(LANES),
                   pl.BlockSpec((N_EXPERTS, LANES), lambda i: (0, 0))],
        out_shape=[jax.ShapeDtypeStruct((t, D_MODEL), F32), jax.ShapeDtypeStruct((t, D_MODEL), BF16),
                   jax.ShapeDtypeStruct((8, t), jnp.int32), jax.ShapeDtypeStruct((8, t), jnp.int32),
                   jax.ShapeDtypeStruct((t, LANES), F32), jax.ShapeDtypeStruct((N_EXPERTS, LANES), F32)],
        compiler_params=_cparams(("arbitrary",)), name="out_proj_router",
    )(*srcs, *atts, pool_out, y_f, y_b, xs, z, *consts)


def _expert_kernel(blk_ref, e_ref, lo_ref, hi_ref, first_ref, newe_ref, x_ref, w1_ref, b1g_ref, b1l_ref,
                   w2_ref, b2_ref, perm_ref, o_ref, w1g_s, w1l_s, w2_s):
    w = pl.program_id(0)
    lo = lo_ref[w]
    hi = hi_ref[w]

    @pl.when(newe_ref[w] == 1)
    def _():
        for j in range(D_FF // LANES):
            chunk = w1_ref[:, 2 * LANES * j:2 * LANES * (j + 1)].astype(BF16)
            r = _dot(chunk, perm_ref[...])
            w1g_s[:, LANES * j:LANES * (j + 1)] = r[:, :LANES].astype(BF16)
            w1l_s[:, LANES * j:LANES * (j + 1)] = r[:, LANES:].astype(BF16)
        w2_s[...] = w2_ref[...].astype(BF16)

    @pl.when(hi > lo)
    def _():
        x = x_ref[...]
        acc = jnp.zeros(o_ref.shape, F32) + b2_ref[...]
        for c in range(0, D_FF, FF_CHUNK):
            hg = _dot(x, w1g_s[:, c:c + FF_CHUNK]) + b1g_ref[:, c:c + FF_CHUNK]
            hl = _dot(x, w1l_s[:, c:c + FF_CHUNK]) + b1l_ref[:, c:c + FF_CHUNK]
            hg = jnp.minimum(hg, SWIGLU_LIMIT)
            hl = jnp.clip(hl, -SWIGLU_LIMIT, SWIGLU_LIMIT)
            act = hg * jax.nn.sigmoid(SWIGLU_ALPHA * hg) * (hl + 1.0)
            acc = acc + _dot(act.astype(BF16), w2_s[c:c + FF_CHUNK, :])
        rows = blk_ref[w] * MOE_BM + lax.broadcasted_iota(jnp.int32, (MOE_BM, 1), 0)
        mine = (rows >= lo) & (rows < hi)
        y = acc.astype(o_ref.dtype)

        @pl.when(first_ref[w] == 1)
        def _():
            o_ref[...] = jnp.where(mine, y, jnp.zeros_like(y))

        @pl.when(first_ref[w] == 0)
        def _():
            o_ref[...] = jnp.where(mine, y, o_ref[...])


def _experts(x_sorted, meta, layer, w1, b1g, b1l, w2, b2):
    n_rows = x_sorted.shape[0]
    bm = MOE_BM
    n_items = meta[0].shape[0]
    o = np.arange(2 * LANES)[None, :]
    c = np.arange(2 * LANES)[:, None]
    perm = jnp.asarray(np.where(o < LANES, c == 2 * o, c == 2 * (o - LANES) + 1), BF16)
    wspec = lambda r, c: pl.BlockSpec((None, None, r, c), lambda w, blk, e, *_: (layer, e[w], 0, 0))
    bspec = lambda c: pl.BlockSpec((None, 1, c), lambda w, blk, e, *_: (e[w], 0, 0))
    grid_spec = pltpu.PrefetchScalarGridSpec(
        num_scalar_prefetch=6, grid=(n_items,),
        in_specs=[pl.BlockSpec((bm, D_MODEL), lambda w, blk, *_: (blk[w], 0)),
                  wspec(D_MODEL, 2 * D_FF), bspec(D_FF), bspec(D_FF), wspec(D_FF, D_MODEL), bspec(D_MODEL),
                  pl.BlockSpec(perm.shape, lambda w, *_: (0, 0))],
        out_specs=pl.BlockSpec((bm, D_MODEL), lambda w, blk, *_: (blk[w], 0)),
        scratch_shapes=[pltpu.VMEM((D_MODEL, D_FF), BF16), pltpu.VMEM((D_MODEL, D_FF), BF16),
                        pltpu.VMEM((D_FF, D_MODEL), BF16)])
    return pl.pallas_call(
        _expert_kernel, grid_spec=grid_spec,
        out_shape=jax.ShapeDtypeStruct((n_rows, D_MODEL), BF16),
        compiler_params=_cparams(("arbitrary",)), name="moe_experts",
    )(*meta, x_sorted, w1, b1g, b1l, w2, b2, perm)


def _combine_kernel(h_ref, y_ref, gate_ref, g_ref, b_ref, o_ref, *, alpha):
    gates = gate_ref[...]
    f = y_ref[0].astype(F32) * gates[:, 0:1]
    for k in range(1, TOP_K):
        f = f + y_ref[k].astype(F32) * gates[:, k:k + 1]
    o_ref[...] = _layer_norm(alpha * h_ref[...] + f, g_ref[...], b_ref[...])


def _combine_ln(h1, y_tok, gate_col, ln_g, ln_b, alpha, tok0, n_tok):
    tm = TM_OUT
    assert tok0 % tm == 0 and n_tok % tm == 0
    b0 = tok0 // tm
    return pl.pallas_call(
        functools.partial(_combine_kernel, alpha=alpha),
        grid=(n_tok // tm,),
        in_specs=[pl.BlockSpec((tm, D_MODEL), lambda i: (b0 + i, 0)),
                  pl.BlockSpec((TOP_K, tm, D_MODEL), lambda i: (0, b0 + i, 0)),
                  pl.BlockSpec((tm, LANES), lambda i: (b0 + i, 0)),
                  pl.BlockSpec((1, D_MODEL), lambda i: (0, 0)),
                  pl.BlockSpec((1, D_MODEL), lambda i: (0, 0))],
        out_specs=pl.BlockSpec((tm, D_MODEL), lambda i: (i, 0)),
        out_shape=jax.ShapeDtypeStruct((n_tok, D_MODEL), F32),
        compiler_params=_cparams(("parallel",)), name="combine_ln",
    )(h1, y_tok, gate_col, ln_g, ln_b)


def _routing(idx_t, rank_t, counts):
    t = idx_t.shape[1]
    n_asg = TOP_K * t
    bm = MOE_BM
    n_blocks = n_asg // bm
    n_items = n_blocks + N_EXPERTS - 1
    i32 = jnp.int32
    idx = idx_t[:TOP_K]
    tok = jnp.arange(t, dtype=i32)[None, :]
    skeys = lax.sort((idx * t + tok).reshape(-1))
    src_tok = skeys % t
    ends = jnp.cumsum(counts.astype(i32))
    starts = ends - counts.astype(i32)
    experts = jnp.arange(N_EXPERTS, dtype=i32)[:, None, None]
    pos = (jnp.sum(jnp.where(idx[None] == experts, starts[:, None, None], 0), axis=0) + rank_t[:TOP_K]).reshape(-1)
    first_tile = starts // bm
    ntiles = jnp.where(ends > starts, (ends + bm - 1) // bm - first_tile, 0)
    cum = jnp.cumsum(ntiles)
    base = cum - ntiles
    total = cum[-1]
    w = jnp.arange(n_items, dtype=i32)
    valid = w < total
    e_w = jnp.minimum(jnp.sum((cum[None, :] <= w[:, None]).astype(i32), axis=1), N_EXPERTS - 1)
    e_last = jnp.sum(jnp.where(w == total - 1, e_w, 0))
    e_w = jnp.where(valid, e_w, e_last)
    onehot_e = e_w[:, None] == jnp.arange(N_EXPERTS, dtype=i32)[None, :]
    at_e = lambda table: jnp.sum(jnp.where(onehot_e, table[None, :], 0), axis=1)
    blk = jnp.where(valid, at_e(first_tile) + (w - at_e(base)), n_blocks - 1).astype(i32)
    lo = jnp.where(valid, jnp.maximum(at_e(starts), blk * bm), 0).astype(i32)
    hi = jnp.where(valid, jnp.minimum(at_e(ends), (blk + 1) * bm), 0).astype(i32)
    one = jnp.ones((1,), i32)
    first = jnp.concatenate([one, (blk[1:] != blk[:-1]).astype(i32)])
    new_e = jnp.concatenate([one, (e_w[1:] != e_w[:-1]).astype(i32)])
    return src_tok, pos, (blk, e_w, lo, hi, first, new_e)


def _prep_layer(i, w_in, conv_w, conv_b, a_log, dt_bias, w_pool, pool_scale, rpb, w_out, w_router, b_router,
                w1, b1, w2, b2, d_skip, ssd_norm_g, ln1_g, ln1_b, ln2_g, ln2_b):
    c_x = POOL_W + SSD_INNER + SSD_XBC
    c_dt = c_x + DT_COLS
    w = w_in[i]
    w_dt = w[:, c_x:c_dt]
    w_q, w_k, w_v = (w[:, c_dt + j * ATT_W:c_dt + (j + 1) * ATT_W] for j in range(3))
    pad = LANES - DT_REP * DT_COLS
    w_all = jnp.concatenate([w[:, :c_x], w_k] + [w_dt] * DT_REP + [jnp.zeros((D_MODEL, pad), F32)],
                            axis=1).astype(BF16)
    w_t = jnp.concatenate([w_q, w_v, w_dt], axis=1).T.astype(BF16)
    packed_row = lambda v: jnp.concatenate([v.reshape(1, DT_COLS)] * DT_REP + [jnp.zeros((1, pad), F32)], axis=1)
    w_bd = jnp.zeros((POOL_W, POOL_W), F32)
    for g in range(POOL_GROUPS):
        sl = slice(g * POOL_GDIM, (g + 1) * POOL_GDIM)
        w_bd = w_bd.at[sl, sl].set(w_pool[i, g])
    conv_w_pad = jnp.concatenate([conv_w[i], jnp.zeros((HALO - SSD_CONV, SSD_XBC), F32)], axis=0)
    return dict(
        layer=i, w1=w1, w2=w2,
        w_all=w_all, w_t=w_t, w_bd=w_bd.astype(BF16), pool_scale=pool_scale[i].reshape(1, POOL_W),
        conv_w=conv_w_pad, conv_b=conv_b[i].reshape(1, SSD_XBC),
        dt_bias_row=packed_row(dt_bias[i]), dt_bias_col=dt_bias[i].reshape(DT_COLS, 1),
        alog_row=packed_row(a_log[i]), alog_col=a_log[i].reshape(DT_COLS, 1),
        bias_tab=_att_bias_table(rpb[i]),
        w_out=w_out[i].astype(BF16), dskip=jnp.repeat(d_skip[i], SSD_HDIM).reshape(1, SSD_INNER),
        norm_g=ssd_norm_g[i].reshape(1, SSD_INNER), ln1_g=ln1_g[i].reshape(1, D_MODEL), ln1_b=ln1_b[i].reshape(1, D_MODEL),
        w_router_t=jnp.concatenate(_split3(w_router[i].T)[:2], axis=0), b_router=b_router[i].reshape(N_EXPERTS, 1),
        b1g=b1[i, :, 0::2].reshape(N_EXPERTS, 1, D_FF), b1l=b1[i, :, 1::2].reshape(N_EXPERTS, 1, D_FF),
        b2=b2[i].reshape(N_EXPERTS, 1, D_MODEL),
        ln2_g=ln2_g[i].reshape(1, D_MODEL), ln2_b=ln2_b[i].reshape(1, D_MODEL))


def _encoder_layer(srcs, p, seq_groups, alpha, split_out):
    t = sum(a.shape[0] for a in srcs)
    seq_lens = tuple(s for tok0, n, s in seq_groups for _ in range(n))
    u, z, xbc, k, dt, qt, vt, dtt = _in_proj(srcs, p['w_all'], p['w_t'])
    pool_out = _pool_mixer(u, p['w_bd'], p['pool_scale'], seq_lens)
    xs, xcf, xcb, c, bt = _conv_silu(xbc, dt, p['conv_w'], p['conv_b'], p['dt_bias_row'], seq_lens)
    y_f, y_b = _ssd_scan(xcf, xcb, c, bt, dt, dtt, p['dt_bias_row'], p['dt_bias_col'], p['alog_row'], p['alog_col'],
                         seq_lens)
    atts = [_attention(qt, k, vt, p['bias_tab'], tok0, n, s) for tok0, n, s in seq_groups]
    h1, h1b, idx_t, rank_t, gate_col, cnt = _out_proj_router(
        srcs, pool_out, y_f, y_b, xs, z, atts, p['w_out'], p['dskip'], p['norm_g'], p['ln1_g'], p['ln1_b'],
        p['w_router_t'], p['b_router'], alpha)
    src_tok, pos, meta = _routing(idx_t, rank_t, cnt[:, 0])
    x_sorted = jnp.take(h1b, src_tok, axis=0, mode='clip')
    y_sorted = _experts(x_sorted, meta, p['layer'], p['w1'], p['b1g'], p['b1l'], p['w2'], p['b2'])
    y_tok = jnp.take(y_sorted, pos, axis=0, mode='clip').reshape(TOP_K, t, D_MODEL)
    ranges = [(tok0, n * s) for tok0, n, s in seq_groups] if split_out else [(0, t)]
    return [_combine_ln(h1, y_tok, gate_col, p['ln2_g'], p['ln2_b'], alpha, tok0, n_tok) for tok0, n_tok in ranges]


def kernel(x_prompt, x_sample, w_in, conv_w, conv_b, a_log, dt_bias, d_skip, ssd_norm_g, w_pool, pool_scale, rpb, w_out,
           ln1_g, ln1_b, w_router, b_router, w1, b1, w2, b2, ln2_g, ln2_b):
    depth = w_in.shape[0]
    alpha = (2 * depth) ** 0.25
    bp, sp, _ = x_prompt.shape
    bs, ss, _ = x_sample.shape
    seq_groups = ((0, bp, sp), (bp * sp, bs, ss))
    outs = [x_prompt.reshape(bp * sp, D_MODEL), x_sample.reshape(bs * ss, D_MODEL)]
    for i in range(depth):
        p = _prep_layer(i, w_in, conv_w, conv_b, a_log, dt_bias, w_pool, pool_scale, rpb, w_out, w_router, b_router,
                        w1, b1, w2, b2, d_skip, ssd_norm_g, ln1_g, ln1_b, ln2_g, ln2_b)
        outs = _encoder_layer(outs, p, seq_groups, alpha, split_out=(i == depth - 1))
    return (outs[0].reshape(bp, sp, D_MODEL), outs[1].reshape(bs, ss, D_MODEL))
```
